```python
import jax
import jax.numpy as jnp
from jax import lax
import numpy as np

D_MODEL = 2048
BATCH = 16
SEQ = 2048
DEPTH = 4

GRID_W = 64
CTX_LEN = 256
N_MIXERS = 2
N_RG = (DEPTH + 1) // 2
N_FN = DEPTH // 2
LRU_WIDTH = D_MODEL
LRU_HEADS = 8
LRU_HEAD_DIM = LRU_WIDTH // LRU_HEADS
CONV_W = 4
CONV_LEFT = 2
RG_C = 8.0
FN_GROUPS = 8
FN_GROUP_DIM = D_MODEL // FN_GROUPS
N_EXPERTS = 64
TOP_K = 8
N_GROUPS = 8
TOPK_GROUPS = 4
EXPERT_DIM = D_MODEL // 8
ROUTED_SCALE = 2.5
N_MOD = 6
EPS = 1e-6
POS_THETA = 10000.0

kernel_name = 'hybrid_rglru_fourier_moe_dit'


def rmsnorm(x, g):
    xf = x.astype(jnp.float32)
    y = xf * lax.rsqrt(jnp.mean(xf * xf, axis=-1, keepdims=True) + EPS)
    return (y * g.astype(jnp.float32)).astype(x.dtype)


def ada_params(cond, w_mod, b_mod):
    m = jax.nn.silu(cond) @ w_mod + b_mod
    return [p[:, None, :] for p in jnp.split(m, N_MOD, axis=-1)]


def modulate(h, shift, scale):
    return h * (1.0 + scale) + shift


def grid_pos_embed(rows, dim, dtype):
    row = jnp.repeat(jnp.arange(rows, dtype=jnp.float32), GRID_W)
    col = jnp.tile(jnp.arange(GRID_W, dtype=jnp.float32), rows)
    quarter = dim // 4
    freqs = POS_THETA ** (-jnp.arange(quarter, dtype=jnp.float32) / quarter)
    ar = row[:, None] * freqs
    ac = col[:, None] * freqs
    emb = jnp.concatenate([jnp.sin(ar), jnp.cos(ar), jnp.sin(ac), jnp.cos(ac)], axis=-1)
    return emb.astype(dtype)


def centred_dwconv(u, w, b):
    t = u.shape[1]
    up = jnp.pad(u, ((0, 0), (CONV_LEFT, CONV_W - 1 - CONV_LEFT), (0, 0)))
    y = up[:, 0:t] * w[0]
    for k in range(1, CONV_W):
        y = y + up[:, k:k + t] * w[k]
    return y + b


def rglru_coeffs(u, w_a, b_a, w_i, b_i, lam):
    bsz, t, width = u.shape
    uh = u.reshape(bsz, t, LRU_HEADS, LRU_HEAD_DIM)
    r = jax.nn.sigmoid(jnp.einsum('bthi,hij->bthj', uh, w_a).reshape(bsz, t, width) + b_a)
    gi = jax.nn.sigmoid(jnp.einsum('bthi,hij->bthj', uh, w_i).reshape(bsz, t, width) + b_i)
    log_a = RG_C * r.astype(jnp.float32) * jax.nn.log_sigmoid(lam.astype(jnp.float32))
    a = jnp.exp(log_a)
    b = jnp.sqrt(-jnp.expm1(2.0 * log_a)) * (gi * u).astype(jnp.float32)
    return a, b


def linear_scan(a, b, h0, reverse):
    if reverse:
        a, b = jnp.flip(a, 1), jnp.flip(b, 1)

    def combine(l, r):
        return l[0] * r[0], r[0] * l[1] + r[1]

    a_cum, h = lax.associative_scan(combine, (a, b), axis=1)
    h = h + a_cum * h0[:, None, :]
    return jnp.flip(h, 1) if reverse else h


def griffin_mix(h_lat, h_ctx, w_in, conv_w, conv_b, w_a, b_a, w_i, b_i, lam, w_out):
    def branches(h):
        gate_br, rec = jnp.split(h @ w_in, 2, axis=-1)
        return jax.nn.gelu(gate_br), centred_dwconv(rec, conv_w, conv_b)

    gate_lat, u_lat = branches(h_lat)
    gate_ctx, u_ctx = branches(h_ctx)
    zero_state = jnp.zeros((u_ctx.shape[0], LRU_WIDTH), jnp.float32)
    ys_lat, ys_ctx = [], []
    for d in range(2):
        reverse = d == 1
        a_c, b_c = rglru_coeffs(u_ctx, w_a[d], b_a[d], w_i[d], b_i[d], lam[d])
        a_l, b_l = rglru_coeffs(u_lat, w_a[d], b_a[d], w_i[d], b_i[d], lam[d])
        hc = linear_scan(a_c, b_c, zero_state, reverse)
        h_final = hc[:, 0] if reverse else hc[:, -1]
        hl = linear_scan(a_l, b_l, h_final, reverse)
        ys_ctx.append(hc)
        ys_lat.append(hl)
    y_lat = (ys_lat[0] + ys_lat[1]).astype(h_lat.dtype)
    y_ctx = (ys_ctx[0] + ys_ctx[1]).astype(h_ctx.dtype)
    return (y_lat * gate_lat) @ w_out, (y_ctx * gate_ctx) @ w_out


def fourier_mix(h, w_out):
    bsz, t, dim = h.shape
    hg = h.astype(jnp.float32).reshape(bsz, t, FN_GROUPS, FN_GROUP_DIM)
    f = jnp.fft.fft2(hg, axes=(1, 3), norm='ortho').real
    return f.reshape(bsz, t, dim).astype(h.dtype) @ w_out


def swiglu(t, wg, wu, wd):
    return (jax.nn.silu(t @ wg) * (t @ wu)) @ wd


def moe(h, router_w, router_b, w_gate, w_up, w_down, s_gate, s_up, s_down):
    shape = h.shape
    t = h.reshape(-1, shape[-1])
    n_tok = t.shape[0]
    scores = jax.nn.sigmoid(t.astype(jnp.float32) @ router_w.astype(jnp.float32))
    biased = scores + router_b.astype(jnp.float32)
    grouped = biased.reshape(n_tok, N_GROUPS, N_EXPERTS // N_GROUPS)
    group_score = lax.top_k(grouped, 2)[0].sum(-1)
    _, top_groups = lax.top_k(group_score, TOPK_GROUPS)
    group_mask = jax.nn.one_hot(top_groups, N_GROUPS, dtype=jnp.float32).sum(1)
    expert_mask = jnp.repeat(group_mask, N_EXPERTS // N_GROUPS, axis=1)
    masked = jnp.where(expert_mask > 0, biased, -jnp.inf)
    _, top_idx = lax.top_k(masked, TOP_K)
    w = jnp.take_along_axis(scores, top_idx, axis=1)
    w = w / jnp.sum(w, axis=-1, keepdims=True) * ROUTED_SCALE
    gates = jnp.einsum('tk,tke->et', w, jax.nn.one_hot(top_idx, N_EXPERTS, dtype=jnp.float32))

    def expert_step(acc, xs):
        wg, wu, wd, g = xs
        return acc + g[:, None].astype(t.dtype) * swiglu(t, wg, wu, wd), None

    routed, _ = lax.scan(expert_step, jnp.zeros_like(t), (w_gate, w_up, w_down, gates))
    return (routed + swiglu(t, s_gate, s_up, s_down)).reshape(shape)


def setup_inputs(seed: int = 0) -> dict:
    key = jax.random.key(seed)
    ks = jax.random.split(key, 27)
    f32 = jnp.float32
    D, W, E, F = D_MODEL, LRU_WIDTH, N_EXPERTS, EXPERT_DIM

    def nrm(k, shape, scale):
        return jax.random.normal(k, shape, f32) * scale

    u = jax.random.uniform(ks[16], (N_RG, 2, W), f32, 0.9, 0.999)
    a_base = u ** (1.0 / RG_C)
    rg_lam = jnp.log(a_base) - jnp.log1p(-a_base)
    return {
        'x': nrm(ks[0], (BATCH, SEQ, D), 1.0),
        'c': nrm(ks[1], (BATCH, D), 1.0),
        'ctx': nrm(ks[2], (BATCH, CTX_LEN, D), 1.0),
        'c_ctx': nrm(ks[3], (D,), 1.0),
        'norm1_g': 1.0 + nrm(ks[4], (DEPTH, D), 0.02),
        'norm2_g': 1.0 + nrm(ks[5], (DEPTH, D), 0.02),
        'final_norm_g': 1.0 + nrm(ks[6], (D,), 0.02),
        'w_mod': nrm(ks[7], (DEPTH, D, N_MOD * D), 0.5 * D ** -0.5),
        'b_mod': nrm(ks[8], (DEPTH, N_MOD * D), 0.02),
        'rg_w_in': nrm(ks[9], (N_RG, D, 2 * W), D ** -0.5),
        'rg_conv_w': nrm(ks[10], (N_RG, CONV_W, W), CONV_W ** -0.5),
        'rg_conv_b': nrm(ks[11], (N_RG, W), 0.02),
        'rg_w_a': nrm(ks[12], (N_RG, 2, LRU_HEADS, LRU_HEAD_DIM, LRU_HEAD_DIM), LRU_HEAD_DIM ** -0.5),
        'rg_b_a': nrm(ks[13], (N_RG, 2, W), 0.02),
        'rg_w_i': nrm(ks[14], (N_RG, 2, LRU_HEADS, LRU_HEAD_DIM, LRU_HEAD_DIM), LRU_HEAD_DIM ** -0.5),
        'rg_b_i': nrm(ks[15], (N_RG, 2, W), 0.02),
        'rg_lam': rg_lam,
        'rg_w_out': nrm(ks[17], (N_RG, W, D), W ** -0.5),
        'fn_w_out': nrm(ks[18], (N_FN, D, D), D ** -0.5),
        'router_w': nrm(ks[19], (DEPTH, D, E), D ** -0.5),
        'router_b': nrm(ks[20], (DEPTH, E), 0.01),
        'exp_w_gate': nrm(ks[21], (DEPTH, E, D, F), D ** -0.5),
        'exp_w_up': nrm(ks[22], (DEPTH, E, D, F), D ** -0.5),
        'exp_w_down': nrm(ks[23], (DEPTH, E, F, D), F ** -0.5),
        'sh_w_gate': nrm(ks[24], (DEPTH, D, F), D ** -0.5),
        'sh_w_up': nrm(ks[25], (DEPTH, D, F), D ** -0.5),
        'sh_w_down': nrm(ks[26], (DEPTH, F, D), F ** -0.5),
    }


def reference(x, c, ctx, c_ctx, norm1_g, norm2_g, final_norm_g, w_mod, b_mod,
              rg_w_in, rg_conv_w, rg_conv_b, rg_w_a, rg_b_a, rg_w_i, rg_b_i, rg_lam, rg_w_out,
              fn_w_out, router_w, router_b, exp_w_gate, exp_w_up, exp_w_down,
              sh_w_gate, sh_w_up, sh_w_down):
    n_ctx = ctx.shape[1]
    rows = x.shape[1] // GRID_W
    x = x + grid_pos_embed(rows, x.shape[-1], x.dtype)[None]
    xc = ctx
    for i in range(DEPTH):
        last = i == DEPTH - 1
        j = i // N_MIXERS
        sh1, sc1, g1, sh2, sc2, g2 = ada_params(c, w_mod[i], b_mod[i])
        csh1, csc1, cg1, csh2, csc2, cg2 = ada_params(c_ctx[None], w_mod[i], b_mod[i])
        h_lat = modulate(rmsnorm(x, norm1_g[i]), sh1, sc1)
        h_ctx = modulate(rmsnorm(xc, norm1_g[i]), csh1, csc1)
        if i % N_MIXERS == 0:
            y_lat, y_ctx = griffin_mix(h_lat, h_ctx, rg_w_in[j], rg_conv_w[j], rg_conv_b[j],
                                       rg_w_a[j], rg_b_a[j], rg_w_i[j], rg_b_i[j], rg_lam[j],
                                       rg_w_out[j])
        else:
            y_lat = fourier_mix(h_lat, fn_w_out[j])
            y_ctx = None if last else fourier_mix(h_ctx, fn_w_out[j])
        x = x + g1 * y_lat
        moe_args = (router_w[i], router_b[i], exp_w_gate[i], exp_w_up[i], exp_w_down[i],
                    sh_w_gate[i], sh_w_up[i], sh_w_down[i])
        if last:
            x = x + g2 * moe(modulate(rmsnorm(x, norm2_g[i]), sh2, sc2), *moe_args)
        else:
            xc = xc + cg1 * y_ctx
            f_lat = modulate(rmsnorm(x, norm2_g[i]), sh2, sc2)
            f_ctx = modulate(rmsnorm(xc, norm2_g[i]), csh2, csc2)
            y = moe(jnp.concatenate([f_ctx, f_lat], axis=1), *moe_args)
            xc = xc + cg2 * y[:, :n_ctx]
            x = x + g2 * y[:, n_ctx:]
    return rmsnorm(x, final_norm_g)
```

```python
import functools
import math

import numpy as np
import jax
import jax.numpy as jnp
from jax import lax
from jax.experimental import pallas as pl
from jax.experimental.pallas import tpu as pltpu

GRID_W = 64
N_MIXERS = 2
LRU_HEADS = 8
CONV_W = 4
CONV_LEFT = 2
RG_C = 8.0
FN_GROUPS = 8
TOP_K = 8
N_GROUPS = 8
TOPK_GROUPS = 4
ROUTED_SCALE = 2.5
N_MOD = 6
EPS = 1e-6
POS_THETA = 10000.0

V7X_VMEM_BYTES = 64 * 1024 * 1024
V7X_SUBLANES = 8
V7X_BF16_ROWS = 16
V7X_MXU_DIM = 256
MAX_ROW_TILE = 256
VMEM_LIMIT = 56 * 1024 * 1024

F32 = jnp.float32
BF16 = jnp.bfloat16


def _params(sem, vmem=VMEM_LIMIT):
    return pltpu.CompilerParams(dimension_semantics=sem, vmem_limit_bytes=vmem)


def _resident(shape, index_map):
    return pl.BlockSpec(shape, index_map, pipeline_mode=pl.Buffered(1))


def _col_chunk(n, cap=512):
    c = min(n, cap)
    while n % c:
        c //= 2
    return c


def _normmod(x, g, shift, scale):
    ms = jnp.mean(x * x, axis=-1, keepdims=True)
    y = x * lax.rsqrt(ms + EPS) * g
    return y * (1.0 + scale) + shift


def _nt_dot(a, b):
    return lax.dot_general(a, b, (((1,), (1,)), ((), ())), preferred_element_type=F32)


class _Layout:
    def __init__(self, batch, t_ctx, t_lat, with_ctx=True):
        g = math.gcd(t_ctx, t_lat)
        r = min(g, MAX_ROW_TILE)
        while g % r or r % V7X_BF16_ROWS:
            r -= 1
        assert r >= V7X_BF16_ROWS
        self.R = r
        self.B = batch
        self.Tc = t_ctx
        self.T = t_lat
        self.nC = t_ctx // r if with_ctx else 0
        self.nL = t_lat // r
        self.nct = batch * self.nC
        self.ntiles = self.nct + batch * self.nL
        self.N = self.ntiles * r
        assert (batch * t_ctx) % t_lat == 0 or not with_ctx

    def seg(self, i, tile=None):
        tile = self.R if tile is None else tile
        nct = self.nct * self.R // tile
        n_l = self.T // tile
        return jnp.where(i < nct, 0, 1 + (i - nct) // n_l)


def _mod_spec(lay, d, which, tile=None):
    return pl.BlockSpec((None, 1, d), lambda i, *_: (lay.seg(i, tile), 0, which))


def _ada_kernel(cond_ref, w_ref, b_ref, out_ref):
    s = jax.nn.silu(cond_ref[...]).astype(BF16)
    out_ref[...] = jnp.dot(s, w_ref[...].astype(BF16), preferred_element_type=F32) + b_ref[...]


def _ada(cond, w_mod, b_mod):
    depth, d, nm = w_mod.shape
    rows = cond.shape[0]
    tn = _col_chunk(nm, 1024)
    return pl.pallas_call(
        _ada_kernel,
        grid=(depth, nm // tn),
        in_specs=[
            pl.BlockSpec((rows, d), lambda l, j: (0, 0)),
            pl.BlockSpec((None, d, tn), lambda l, j: (l, 0, j)),
            pl.BlockSpec((None, 1, tn), lambda l, j: (l, 0, j)),
        ],
        out_specs=pl.BlockSpec((None, rows, tn), lambda l, j: (l, 0, j)),
        out_shape=jax.ShapeDtypeStruct((depth, rows, nm), F32),
        compiler_params=_params(("arbitrary", "arbitrary")),
        name="ada_mod",
    )(cond, w_mod, b_mod.reshape(depth, 1, nm))


def _addpos_kernel(x_ref, pos_ref, out_ref, *, nct):
    i = pl.program_id(0)
    scale = jnp.where(i >= nct, 1.0, 0.0).astype(F32)
    out_ref[...] = x_ref[...] + pos_ref[...] * scale


def _addpos(lay, xall, pos):
    d = xall.shape[1]
    r = lay.R
    return pl.pallas_call(
        functools.partial(_addpos_kernel, nct=lay.nct),
        grid=(lay.ntiles,),
        in_specs=[
            pl.BlockSpec((r, d), lambda i: (i, 0)),
            pl.BlockSpec((r, d), lambda i: (jnp.where(i < lay.nct, 0, (i - lay.nct) % lay.nL), 0)),
        ],
        out_specs=pl.BlockSpec((r, d), lambda i: (i, 0)),
        out_shape=jax.ShapeDtypeStruct(xall.shape, F32),
        compiler_params=_params(("arbitrary",)),
        name="add_pos",
    )(xall, pos)


def _rg1_kernel(x_ref, g_ref, sh_ref, sc_ref, w_ref, gate_ref, rec_ref, *, width, cn):
    h = _normmod(x_ref[...], g_ref[...], sh_ref[...], sc_ref[...]).astype(BF16)
    for c in range(2 * width // cn):
        z = jnp.dot(h, w_ref[:, c * cn:(c + 1) * cn], preferred_element_type=F32)
        if c * cn < width:
            gate_ref[:, c * cn:(c + 1) * cn] = jax.nn.gelu(z).astype(BF16)
        else:
            rec_ref[:, c * cn - width:(c + 1) * cn - width] = z.astype(BF16)


def _rg1(lay, x, g, mods, w_in):
    d = x.shape[1]
    width = w_in.shape[1] // 2
    r = lay.R
    cn = _col_chunk(width)
    row = lambda i: (i, 0)
    return pl.pallas_call(
        functools.partial(_rg1_kernel, width=width, cn=cn),
        grid=(lay.ntiles,),
        in_specs=[
            pl.BlockSpec((r, d), row),
            _resident((1, d), lambda i: (0, 0)),
            _mod_spec(lay, d, 0),
            _mod_spec(lay, d, 1),
            _resident((d, 2 * width), lambda i: (0, 0)),
        ],
        out_specs=[pl.BlockSpec((r, width), row), pl.BlockSpec((r, width), row)],
        out_shape=[jax.ShapeDtypeStruct((lay.N, width), BF16)] * 2,
        compiler_params=_params(("arbitrary",)),
        name="rg_in_proj",
    )(x, g, mods, mods, w_in)


def _log_sigmoid(x):
    return jnp.minimum(x, 0.0) - jnp.log1p(jnp.exp(-jnp.abs(x)))


def _neg_expm1(t, exp_t):
    series = t * (1.0 + t * (0.5 + t * (1.0 / 6.0 + t * (1.0 / 24.0 + t * (1.0 / 120.0)))))
    return -jnp.where(t > -0.1, series, exp_t - 1.0)


def _rg2_kernel(cur_ref, prev_ref, next_ref, cw_ref, cb_ref, wa_ref, wi_ref, ba_ref, bi_ref,
                lam_ref, out_ref, carry_ref, *, rev, n_c, n_l, r, hd, heads):
    s = pl.program_id(1)

    @pl.when(s == 0)
    def _():
        carry_ref[...] = jnp.zeros_like(carry_ref)

    in_ctx = s < n_c
    pos = jnp.where(in_ctx, s, s - n_c)
    n_seq = jnp.where(in_ctx, n_c, n_l)
    chunk = (n_seq - 1 - pos) if rev else pos
    keep_prev = jnp.where(chunk == 0, 0.0, 1.0).astype(F32)
    keep_next = jnp.where(chunk == n_seq - 1, 0.0, 1.0).astype(F32)

    row = lax.broadcasted_iota(jnp.int32, (r, hd), 0)
    sub = row % V7X_SUBLANES
    hrows = V7X_BF16_ROWS
    groups = r // V7X_SUBLANES

    for h in range(heads):
        sl = slice(h * hd, (h + 1) * hd)
        cur = cur_ref[:, sl].astype(F32)
        pv = prev_ref[:, sl].astype(F32) * keep_prev
        nx = next_ref[:, sl].astype(F32) * keep_next
        p_m2 = pv[hrows - 2:hrows - 1, :]
        p_m1 = pv[hrows - 1:hrows, :]
        n_p1 = nx[0:1, :]
        x_m1 = jnp.where(row == 0, p_m1, pltpu.roll(cur, 1, 0))
        x_m2 = jnp.where(row == 0, p_m2, jnp.where(row == 1, p_m1, pltpu.roll(cur, 2, 0)))
        x_p1 = jnp.where(row == r - 1, n_p1, pltpu.roll(cur, r - 1, 0))
        u = (cw_ref[0:1, sl] * x_m2 + cw_ref[1:2, sl] * x_m1 + cw_ref[2:3, sl] * cur
             + cw_ref[3:4, sl] * x_p1 + cb_ref[:, sl])
        ub = u.astype(BF16)
        ra = jnp.dot(ub, wa_ref[h], preferred_element_type=F32) + ba_ref[:, sl]
        ri = jnp.dot(ub, wi_ref[h], preferred_element_type=F32) + bi_ref[:, sl]
        log_a = RG_C * jax.nn.sigmoid(ra) * _log_sigmoid(lam_ref[:, sl])
        a = jnp.exp(log_a)
        b = jnp.sqrt(_neg_expm1(2.0 * log_a, a * a)) * (jax.nn.sigmoid(ri) * u)

        for sft in (1, 2, 4):
            if rev:
                a_s = pltpu.roll(a, r - sft, 0)
                b_s = pltpu.roll(b, r - sft, 0)
                m = sub < V7X_SUBLANES - sft
            else:
                a_s = pltpu.roll(a, sft, 0)
                b_s = pltpu.roll(b, sft, 0)
                m = sub >= sft
            b = jnp.where(m, a * b_s + b, b)
            a = jnp.where(m, a * a_s, a)

        carry = carry_ref[:, sl]
        outs = [None] * groups
        order = range(groups - 1, -1, -1) if rev else range(groups)
        for gi in order:
            lo = gi * V7X_SUBLANES
            hg = b[lo:lo + V7X_SUBLANES] + a[lo:lo + V7X_SUBLANES] * carry
            carry = hg[0:1] if rev else hg[V7X_SUBLANES - 1:V7X_SUBLANES]
            outs[gi] = hg
        carry_ref[:, sl] = carry
        out_ref[:, sl] = jnp.concatenate(outs, axis=0).astype(BF16)


def _rg2(lay, rec, conv_w, conv_b, w_a, w_i, b_a, b_i, lam, rev):
    n, width = rec.shape
    r, n_c, n_l = lay.R, lay.nC, lay.nL
    heads = w_a.shape[0]
    hd = width // heads
    hr = V7X_BF16_ROWS
    rb = r // hr
    last_hb = n // hr - 1

    def tile(b, s):
        if rev:
            c_t = b * n_c + (n_c - 1 - s)
            l_t = lay.nct + b * n_l + (n_l - 1 - (s - n_c))
        else:
            c_t = b * n_c + s
            l_t = lay.nct + b * n_l + (s - n_c)
        return jnp.where(s < n_c, c_t, l_t)

    const2 = lambda b, s: (0, 0)
    const3 = lambda b, s: (0, 0, 0)
    return pl.pallas_call(
        functools.partial(_rg2_kernel, rev=rev, n_c=n_c, n_l=n_l, r=r, hd=hd, heads=heads),
        grid=(lay.B, n_c + n_l),
        in_specs=[
            pl.BlockSpec((r, width), lambda b, s: (tile(b, s), 0)),
            pl.BlockSpec((hr, width), lambda b, s: (jnp.maximum(tile(b, s) * rb - 1, 0), 0)),
            pl.BlockSpec((hr, width), lambda b, s: (jnp.minimum((tile(b, s) + 1) * rb, last_hb), 0)),
            _resident((CONV_W, width), const2),
            _resident((1, width), const2),
            _resident((heads, hd, hd), const3),
            _resident((heads, hd, hd), const3),
            _resident((1, width), const2),
            _resident((1, width), const2),
            _resident((1, width), const2),
        ],
        out_specs=pl.BlockSpec((r, width), lambda b, s: (tile(b, s), 0)),
        out_shape=jax.ShapeDtypeStruct((n, width), BF16),
        scratch_shapes=[pltpu.VMEM((1, width), F32)],
        compiler_params=_params(("arbitrary", "arbitrary")),
        name="rg_scan_bwd" if rev else "rg_scan_fwd",
    )(rec, rec, rec, conv_w, conv_b, w_a, w_i, b_a, b_i, lam)


def _proj_res_kernel(*refs, gated, cn):
    if gated:
        hf_ref, hb_ref, gate_ref, x_ref, g1_ref, w_ref, out_ref = refs
        a = ((hf_ref[...].astype(F32) + hb_ref[...].astype(F32)) * gate_ref[...].astype(F32)).astype(BF16)
    else:
        a_ref, x_ref, g1_ref, w_ref, out_ref = refs
        a = a_ref[...]
    d = out_ref.shape[1]
    for c in range(d // cn):
        cs = slice(c * cn, (c + 1) * cn)
        y = jnp.dot(a, w_ref[:, cs], preferred_element_type=F32)
        out_ref[:, cs] = x_ref[:, cs] + g1_ref[:, cs] * y


def _proj_res(lay, a_list, x, mods, w_out, x_off=0):
    k, d = w_out.shape
    r = lay.R
    cn = _col_chunk(d)
    row = lambda i: (i, 0)
    gated = len(a_list) == 3
    return pl.pallas_call(
        functools.partial(_proj_res_kernel, gated=gated, cn=cn),
        grid=(lay.ntiles,),
        in_specs=[pl.BlockSpec((r, k), row) for _ in a_list] + [
            pl.BlockSpec((r, d), lambda i: (i + x_off, 0)),
            _mod_spec(lay, d, 2),
            _resident((k, d), lambda i: (0, 0)),
        ],
        out_specs=pl.BlockSpec((r, d), row),
        out_shape=jax.ShapeDtypeStruct((lay.N, d), F32),
        compiler_params=_params(("arbitrary",)),
        name="mixer_out_proj",
    )(*a_list, x, mods, w_out)


def _dft_tables(n):
    k = np.arange(n, dtype=np.int64)
    ang = 2.0 * np.pi * ((k[:, None] * k[None, :]) % n).astype(np.float64) / n
    s = 1.0 / np.sqrt(n)
    return (np.cos(ang) * s).astype(np.float32), (np.sin(ang) * s).astype(np.float32)


def _f1_kernel(x_ref, g_ref, sh_ref, sc_ref, cs_ref, zc_ref, zs_ref, *, groups, gd):
    h = _normmod(x_ref[...], g_ref[...], sh_ref[...], sc_ref[...]).astype(BF16)
    for gi in range(groups):
        sl = slice(gi * gd, (gi + 1) * gd)
        z = jnp.dot(h[:, sl], cs_ref[...], preferred_element_type=F32)
        zc_ref[:, sl] = z[:, :gd].astype(BF16)
        zs_ref[:, sl] = z[:, gd:].astype(BF16)


def _f1(lay, x, g, mods, cs, x_off=0):
    d = x.shape[1]
    r = lay.R
    gd = d // FN_GROUPS
    row = lambda i: (i, 0)
    return pl.pallas_call(
        functools.partial(_f1_kernel, groups=FN_GROUPS, gd=gd),
        grid=(lay.ntiles,),
        in_specs=[
            pl.BlockSpec((r, d), lambda i: (i + x_off, 0)),
            _resident((1, d), lambda i: (0, 0)),
            _mod_spec(lay, d, 0),
            _mod_spec(lay, d, 1),
            _resident((gd, 2 * gd), lambda i: (0, 0)),
        ],
        out_specs=[pl.BlockSpec((r, d), row), pl.BlockSpec((r, d), row)],
        out_shape=[jax.ShapeDtypeStruct((lay.N, d), BF16)] * 2,
        compiler_params=_params(("arbitrary",)),
        name="fourier_channel_dft",
    )(x, g, mods, mods, cs)


def _f2_kernel(ct_ref, st_ref, zc_ref, zs_ref, *rest, t, rc):
    out_ref = rest[-1]
    for c in range(t // rc):
        rs = slice(c * rc, (c + 1) * rc)
        y = jnp.dot(ct_ref[rs, :], zc_ref[...], preferred_element_type=F32)
        y = y - jnp.dot(st_ref[rs, :], zs_ref[...], preferred_element_type=F32)
        out_ref[rs, :] = y.astype(BF16)


def _f2(zc, zs, ct, st, batch, t, blk_off, prev=None):
    n, d = zc.shape
    tn = _col_chunk(d)
    rc = _col_chunk(t)
    zmap = lambda b, j: (blk_off + b, j)
    in_specs = [
        _resident((t, t), lambda b, j: (0, 0)),
        _resident((t, t), lambda b, j: (0, 0)),
        pl.BlockSpec((t, tn), zmap),
        pl.BlockSpec((t, tn), zmap),
    ]
    args = [ct, st, zc, zs]
    aliases = {}
    if prev is not None:
        in_specs.append(pl.BlockSpec(memory_space=pl.ANY))
        args.append(prev)
        aliases = {4: 0}
    return pl.pallas_call(
        functools.partial(_f2_kernel, t=t, rc=rc),
        grid=(batch, d // tn),
        in_specs=in_specs,
        out_specs=pl.BlockSpec((t, tn), zmap),
        out_shape=jax.ShapeDtypeStruct((n, d), BF16),
        input_output_aliases=aliases,
        compiler_params=_params(("arbitrary", "arbitrary")),
        name="fourier_time_dft",
    )(*args)


def _router_kernel(x_ref, g_ref, sh_ref, sc_ref, rw_ref, rb_ref, f_ref, gates_ref, *, n_exp):
    f = _normmod(x_ref[...], g_ref[...], sh_ref[...], sc_ref[...])
    f_hi = f.astype(BF16)
    f_lo = (f - f_hi.astype(F32)).astype(BF16)
    f_ref[...] = f_hi
    rw = rw_ref[...]
    rw_hi = rw.astype(BF16)
    rw_lo = (rw - rw_hi.astype(F32)).astype(BF16)
    logits = _nt_dot(rw_hi, f_hi) + (_nt_dot(rw_hi, f_lo) + _nt_dot(rw_lo, f_hi))
    scores = jax.nn.sigmoid(logits)
    biased = scores + rb_ref[...]
    r = scores.shape[1]
    gsz = n_exp // N_GROUPS
    sub = lax.broadcasted_iota(jnp.int32, (gsz, r), 0)
    neg = jnp.float32(-jnp.inf)

    blocks = [biased[gi * gsz:(gi + 1) * gsz] for gi in range(N_GROUPS)]
    gscore = []
    for blk in blocks:
        m1 = jnp.max(blk, axis=0, keepdims=True)
        first = jnp.min(jnp.where(blk == m1, sub, gsz), axis=0, keepdims=True)
        m2 = jnp.max(jnp.where(sub == first, neg, blk), axis=0, keepdims=True)
        gscore.append(m1 + m2)
    masked = []
    for gi in range(N_GROUPS):
        rank = jnp.zeros((1, r), jnp.int32)
        for gj in range(N_GROUPS):
            if gj == gi:
                continue
            ahead = (gscore[gj] >= gscore[gi]) if gj < gi else (gscore[gj] > gscore[gi])
            rank = rank + ahead.astype(jnp.int32)
        masked.append(jnp.where(rank < TOPK_GROUPS, blocks[gi], neg))
    sel_w = []
    for gi in range(N_GROUPS):
        blk = masked[gi]
        rank = jnp.zeros((gsz, r), jnp.int32)
        for gj in range(N_GROUPS):
            for j in range(gsz):
                other = masked[gj][j:j + 1]
                if gj < gi:
                    ahead = other >= blk
                elif gj > gi:
                    ahead = other > blk
                else:
                    ahead = (other > blk) | ((other == blk) & (sub > j))
                rank = rank + ahead.astype(jnp.int32)
        sel_w.append(jnp.where(rank < TOP_K, scores[gi * gsz:(gi + 1) * gsz], 0.0))
    tot = sel_w[0]
    for gi in range(1, N_GROUPS):
        tot = tot + sel_w[gi]
    denom = jnp.sum(tot, axis=0, keepdims=True)
    for gi in range(N_GROUPS):
        gates_ref[gi * gsz:(gi + 1) * gsz, :] = sel_w[gi] / denom * ROUTED_SCALE


def _router(lay, x, g, mods, rw_t, rb):
    d = x.shape[1]
    n_exp = rw_t.shape[0]
    r = lay.R
    row = lambda i: (i, 0)
    return pl.pallas_call(
        functools.partial(_router_kernel, n_exp=n_exp),
        grid=(lay.ntiles,),
        in_specs=[
            pl.BlockSpec((r, d), row),
            _resident((1, d), lambda i: (0, 0)),
            _mod_spec(lay, d, 3),
            _mod_spec(lay, d, 4),
            _resident((n_exp, d), lambda i: (0, 0)),
            _resident((n_exp, 1), lambda i: (0, 0)),
        ],
        out_specs=[pl.BlockSpec((r, d), row), pl.BlockSpec((n_exp, r), lambda i: (0, i))],
        out_shape=[jax.ShapeDtypeStruct((lay.N, d), BF16), jax.ShapeDtypeStruct((n_exp, lay.N), F32)],
        compiler_params=_params(("arbitrary",)),
        name="moe_router",
    )(x, g, mods, mods, rw_t, rb)


def _experts_kernel(f_ref, gates_ref, wgu_ref, wd_ref, x_ref, g2_ref, out_ref, acc_ref, *, n_steps, fdim):
    e = pl.program_id(1)

    @pl.when(e == 0)
    def _():
        acc_ref[...] = jnp.zeros_like(acc_ref)

    hgu = jnp.dot(f_ref[...], wgu_ref[...], preferred_element_type=F32)
    hid = jax.nn.silu(hgu[:, :fdim]) * hgu[:, fdim:]
    gates = gates_ref[...]
    lane = lax.broadcasted_iota(jnp.int32, gates.shape, 1)
    gcol = jnp.sum(jnp.where(lane == e, gates, 0.0), axis=1, keepdims=True)
    acc_ref[...] += jnp.dot((hid * gcol).astype(BF16), wd_ref[...], preferred_element_type=F32)

    @pl.when(e == n_steps - 1)
    def _():
        out_ref[...] = x_ref[...] + g2_ref[...] * acc_ref[...]


def _experts(lay, f, gates, wgu, wd, x, mods, tm):
    n, d = f.shape
    n_steps, _, f2 = wgu.shape
    fdim = f2 // 2
    gl = gates.shape[1]
    row = lambda i, e: (i, 0)
    return pl.pallas_call(
        functools.partial(_experts_kernel, n_steps=n_steps, fdim=fdim),
        grid=(n // tm, n_steps),
        in_specs=[
            pl.BlockSpec((tm, d), row),
            pl.BlockSpec((tm, gl), row),
            pl.BlockSpec((None, d, f2), lambda i, e: (e, 0, 0)),
            pl.BlockSpec((None, fdim, d), lambda i, e: (e, 0, 0)),
            pl.BlockSpec((tm, d), row),
            _mod_spec(lay, d, 5, tile=tm),
        ],
        out_specs=pl.BlockSpec((tm, d), row),
        out_shape=jax.ShapeDtypeStruct((n, d), F32),
        scratch_shapes=[pltpu.VMEM((tm, d), F32)],
        compiler_params=_params(("arbitrary", "arbitrary")),
        name="moe_experts",
    )(f, gates, wgu, wd, x, mods)


def _final_norm_kernel(x_ref, g_ref, out_ref):
    x = x_ref[...]
    ms = jnp.mean(x * x, axis=-1, keepdims=True)
    out_ref[...] = x * lax.rsqrt(ms + EPS) * g_ref[...]


def _final_norm(x, g, r):
    n, d = x.shape
    return pl.pallas_call(
        _final_norm_kernel,
        grid=(n // r,),
        in_specs=[pl.BlockSpec((r, d), lambda i: (i, 0)), _resident((1, d), lambda i: (0, 0))],
        out_specs=pl.BlockSpec((r, d), lambda i: (i, 0)),
        out_shape=jax.ShapeDtypeStruct((n, d), F32),
        compiler_params=_params(("arbitrary",)),
        name="final_norm",
    )(x, g)


def _grid_pos_embed(rows, dim):
    row = np.repeat(np.arange(rows, dtype=np.float32), GRID_W)
    col = np.tile(np.arange(GRID_W, dtype=np.float32), rows)
    quarter = dim // 4
    freqs = jnp.float32(POS_THETA) ** (-jnp.arange(quarter, dtype=F32) / quarter)
    ar = jnp.asarray(row)[:, None] * freqs
    ac = jnp.asarray(col)[:, None] * freqs
    return jnp.concatenate([jnp.sin(ar), jnp.cos(ar), jnp.sin(ac), jnp.cos(ac)], axis=-1).astype(F32)


def _moe_tile(lay):
    tm = 2 * lay.R
    if lay.T % tm or (lay.nct * lay.R) % tm:
        tm = lay.R
    return tm


def kernel(x, c, ctx, c_ctx, norm1_g, norm2_g, final_norm_g, w_mod, b_mod, rg_w_in, rg_conv_w, rg_conv_b, rg_w_a, rg_b_a, rg_w_i, rg_b_i, rg_lam, rg_w_out, fn_w_out, router_w, router_b, exp_w_gate, exp_w_up, exp_w_down, sh_w_gate, sh_w_up, sh_w_down):
    batch, t_lat, d = x.shape
    t_ctx = ctx.shape[1]
    depth = w_mod.shape[0]
    n_exp = router_w.shape[2]
    lay = _Layout(batch, t_ctx, t_lat, with_ctx=True)
    lay_lat = _Layout(batch, t_ctx, t_lat, with_ctx=False)

    n_cond = 1 + batch
    pad = (-n_cond) % V7X_SUBLANES
    cond = jnp.concatenate([c_ctx[None], c, jnp.zeros((pad, d), F32)], axis=0)
    mods_all = _ada(cond, w_mod, b_mod)

    pos = _grid_pos_embed(t_lat // GRID_W, d)
    xall = jnp.concatenate([ctx.reshape(batch * t_ctx, d), x.reshape(batch * t_lat, d)], axis=0)
    xall = _addpos(lay, xall, pos)

    gd = d // FN_GROUPS
    cg, sg = _dft_tables(gd)
    cs_tab = jnp.asarray(np.concatenate([cg, sg], axis=1)).astype(BF16)
    ct_lat, st_lat = (jnp.asarray(a).astype(BF16) for a in _dft_tables(t_lat))
    ct_ctx, st_ctx = (jnp.asarray(a).astype(BF16) for a in _dft_tables(t_ctx))

    for i in range(depth):
        last = i == depth - 1
        j = i // N_MIXERS
        mods = mods_all[i].reshape(mods_all.shape[1], 1, N_MOD * d)
        g1n = norm1_g[i][None]
        g2n = norm2_g[i][None]
        if i % N_MIXERS == 0:
            cur = lay
            gate, rec = _rg1(cur, xall, g1n, mods, rg_w_in[j].astype(BF16))
            hs = []
            for dr in range(2):
                hs.append(_rg2(cur, rec, rg_conv_w[j], rg_conv_b[j][None],
                               rg_w_a[j, dr].astype(BF16), rg_w_i[j, dr].astype(BF16),
                               rg_b_a[j, dr][None], rg_b_i[j, dr][None], rg_lam[j, dr][None],
                               rev=dr == 1))
            xall = _proj_res(cur, [hs[0], hs[1], gate], xall, mods, rg_w_out[j].astype(BF16))
        else:
            cur = lay_lat if last else lay
            x_off = lay.nct if last else 0
            zc, zs = _f1(cur, xall, g1n, mods, cs_tab, x_off=x_off)
            fre = None
            if not last:
                fre = _f2(zc, zs, ct_ctx, st_ctx, batch, t_ctx, 0)
            fre = _f2(zc, zs, ct_lat, st_lat, batch, t_lat, cur.nct * cur.R // t_lat, prev=fre)
            xall = _proj_res(cur, [fre], xall, mods, fn_w_out[j].astype(BF16), x_off=x_off)
        if last and cur is lay:
            xall = xall[lay.nct * lay.R:]
            cur = lay_lat
        f, gates_t = _router(cur, xall, g2n, mods, router_w[i].T, router_b[i][:, None])
        gl = -(-(n_exp + 1) // 128) * 128
        gates = jnp.concatenate(
            [gates_t.T, jnp.ones((cur.N, 1), F32), jnp.zeros((cur.N, gl - n_exp - 1), F32)], axis=1)
        wgu = jnp.concatenate([
            jnp.concatenate([exp_w_gate[i], exp_w_up[i]], axis=-1),
            jnp.concatenate([sh_w_gate[i], sh_w_up[i]], axis=-1)[None]], axis=0).astype(BF16)
        wd = jnp.concatenate([exp_w_down[i], sh_w_down[i][None]], axis=0).astype(BF16)
        xall = _experts(cur, f, gates, wgu, wd, xall, mods, _moe_tile(cur))

    out = _final_norm(xall, final_norm_g[None], lay_lat.R)
    return out.reshape(batch, t_lat, d)
```

```python
import functools
import math

import numpy as np
import jax
import jax.numpy as jnp
from jax import lax
from jax.experimental import pallas as pl
from jax.experimental.pallas import tpu as pltpu

GRID_W = 64
N_MIXERS = 2
LRU_HEADS = 8
CONV_W = 4
CONV_LEFT = 2
RG_C = 8.0
FN_GROUPS = 8
TOP_K = 8
N_GROUPS = 8
TOPK_GROUPS = 4
ROUTED_SCALE = 2.5
N_MOD = 6
EPS = 1e-6
POS_THETA = 10000.0

V7X_VMEM_BYTES = 64 * 1024 * 1024
V7X_SUBLANES = 8
V7X_BF16_ROWS = 16
V7X_MXU_DIM = 256
MAX_ROW_TILE = 256
VMEM_LIMIT = 56 * 1024 * 1024

F32 = jnp.float32
BF16 = jnp.bfloat16


def _params(sem, vmem=VMEM_LIMIT):
    return pltpu.CompilerParams(dimension_semantics=sem, vmem_limit_bytes=vmem)


def _resident(shape, index_map):
    return pl.BlockSpec(shape, index_map, pipeline_mode=pl.Buffered(1))


def _col_chunk(n, cap=512):
    c = min(n, cap)
    while n % c:
        c //= 2
    return c


def _normmod(x, g, shift, scale):
    ms = jnp.mean(x * x, axis=-1, keepdims=True)
    y = x * lax.rsqrt(ms + EPS) * g
    return y * (1.0 + scale) + shift


def _nt_dot(a, b):
    return lax.dot_general(a, b, (((1,), (1,)), ((), ())), preferred_element_type=F32)


class _Layout:
    def __init__(self, batch, t_ctx, t_lat, with_ctx=True):
        g = math.gcd(t_ctx, t_lat)
        r = min(g, MAX_ROW_TILE)
        while g % r or r % V7X_BF16_ROWS:
            r -= 1
        assert r >= V7X_BF16_ROWS
        self.R = r
        self.B = batch
        self.Tc = t_ctx
        self.T = t_lat
        self.nC = t_ctx // r if with_ctx else 0
        self.nL = t_lat // r
        self.nct = batch * self.nC
        self.ntiles = self.nct + batch * self.nL
        self.N = self.ntiles * r
        assert (batch * t_ctx) % t_lat == 0 or not with_ctx

    def seg(self, i, tile=None):
        tile = self.R if tile is None else tile
        nct = self.nct * self.R // tile
        n_l = self.T // tile
        return jnp.where(i < nct, 0, 1 + (i - nct) // n_l)


def _mod_spec(lay, d, which, tile=None):
    return pl.BlockSpec((None, 1, d), lambda i, *_: (lay.seg(i, tile), 0, which))


def _ada_kernel(cond_ref, w_ref, b_ref, out_ref):
    s = jax.nn.silu(cond_ref[...]).astype(BF16)
    out_ref[...] = jnp.dot(s, w_ref[...].astype(BF16), preferred_element_type=F32) + b_ref[...]


def _ada(cond, w_mod, b_mod):
    depth, d, nm = w_mod.shape
    rows = cond.shape[0]
    tn = _col_chunk(nm, 1024)
    return pl.pallas_call(
        _ada_kernel,
        grid=(depth, nm // tn),
        in_specs=[
            pl.BlockSpec((rows, d), lambda l, j: (0, 0)),
            pl.BlockSpec((None, d, tn), lambda l, j: (l, 0, j)),
            pl.BlockSpec((None, 1, tn), lambda l, j: (l, 0, j)),
        ],
        out_specs=pl.BlockSpec((None, rows, tn), lambda l, j: (l, 0, j)),
        out_shape=jax.ShapeDtypeStruct((depth, rows, nm), F32),
        compiler_params=_params(("arbitrary", "arbitrary")),
        name="ada_mod",
    )(cond, w_mod, b_mod.reshape(depth, 1, nm))


def _addpos_kernel(x_ref, pos_ref, out_ref, *, nct):
    i = pl.program_id(0)
    scale = jnp.where(i >= nct, 1.0, 0.0).astype(F32)
    out_ref[...] = x_ref[...] + pos_ref[...] * scale


def _addpos(lay, xall, pos):
    d = xall.shape[1]
    r = lay.R
    return pl.pallas_call(
        functools.partial(_addpos_kernel, nct=lay.nct),
        grid=(lay.ntiles,),
        in_specs=[
            pl.BlockSpec((r, d), lambda i: (i, 0)),
            pl.BlockSpec((r, d), lambda i: (jnp.where(i < lay.nct, 0, (i - lay.nct) % lay.nL), 0)),
        ],
        out_specs=pl.BlockSpec((r, d), lambda i: (i, 0)),
        out_shape=jax.ShapeDtypeStruct(xall.shape, F32),
        compiler_params=_params(("arbitrary",)),
        name="add_pos",
    )(xall, pos)


def _rg1_kernel(x_ref, g_ref, sh_ref, sc_ref, w_ref, gate_ref, rec_ref, *, width, cn):
    h = _normmod(x_ref[...], g_ref[...], sh_ref[...], sc_ref[...]).astype(BF16)
    for c in range(2 * width // cn):
        z = jnp.dot(h, w_ref[:, c * cn:(c + 1) * cn], preferred_element_type=F32)
        if c * cn < width:
            gate_ref[:, c * cn:(c + 1) * cn] = jax.nn.gelu(z).astype(BF16)
        else:
            rec_ref[:, c * cn - width:(c + 1) * cn - width] = z.astype(BF16)


def _rg1(lay, x, g, mods, w_in):
    d = x.shape[1]
    width = w_in.shape[1] // 2
    r = lay.R
    cn = _col_chunk(width)
    row = lambda i: (i, 0)
    return pl.pallas_call(
        functools.partial(_rg1_kernel, width=width, cn=cn),
        grid=(lay.ntiles,),
        in_specs=[
            pl.BlockSpec((r, d), row),
            _resident((1, d), lambda i: (0, 0)),
            _mod_spec(lay, d, 0),
            _mod_spec(lay, d, 1),
            _resident((d, 2 * width), lambda i: (0, 0)),
        ],
        out_specs=[pl.BlockSpec((r, width), row), pl.BlockSpec((r, width), row)],
        out_shape=[jax.ShapeDtypeStruct((lay.N, width), BF16)] * 2,
        compiler_params=_params(("arbitrary",)),
        name="rg_in_proj",
    )(x, g, mods, mods, w_in)


def _log_sigmoid(x):
    return jnp.minimum(x, 0.0) - jnp.log1p(jnp.exp(-jnp.abs(x)))


def _neg_expm1(t, exp_t):
    series = t * (1.0 + t * (0.5 + t * (1.0 / 6.0 + t * (1.0 / 24.0 + t * (1.0 / 120.0)))))
    return -jnp.where(t > -0.1, series, exp_t - 1.0)


def _rg2_kernel(cur_ref, prev_ref, next_ref, cw_ref, cb_ref, wa_ref, wi_ref, ba_ref, bi_ref,
                lam_ref, out_ref, carry_ref, *, rev, n_c, n_l, r, hd, heads):
    s = pl.program_id(1)

    @pl.when(s == 0)
    def _():
        carry_ref[...] = jnp.zeros_like(carry_ref)

    in_ctx = s < n_c
    pos = jnp.where(in_ctx, s, s - n_c)
    n_seq = jnp.where(in_ctx, n_c, n_l)
    chunk = (n_seq - 1 - pos) if rev else pos
    keep_prev = jnp.where(chunk == 0, 0.0, 1.0).astype(F32)
    keep_next = jnp.where(chunk == n_seq - 1, 0.0, 1.0).astype(F32)

    row = lax.broadcasted_iota(jnp.int32, (r, hd), 0)
    sub = row % V7X_SUBLANES
    hrows = V7X_BF16_ROWS
    groups = r // V7X_SUBLANES

    for h in range(heads):
        sl = slice(h * hd, (h + 1) * hd)
        cur = cur_ref[:, sl].astype(F32)
        pv = prev_ref[:, sl].astype(F32) * keep_prev
        nx = next_ref[:, sl].astype(F32) * keep_next
        p_m2 = pv[hrows - 2:hrows - 1, :]
        p_m1 = pv[hrows - 1:hrows, :]
        n_p1 = nx[0:1, :]
        x_m1 = jnp.where(row == 0, p_m1, pltpu.roll(cur, 1, 0))
        x_m2 = jnp.where(row == 0, p_m2, jnp.where(row == 1, p_m1, pltpu.roll(cur, 2, 0)))
        x_p1 = jnp.where(row == r - 1, n_p1, pltpu.roll(cur, r - 1, 0))
        u = (cw_ref[0:1, sl] * x_m2 + cw_ref[1:2, sl] * x_m1 + cw_ref[2:3, sl] * cur
             + cw_ref[3:4, sl] * x_p1 + cb_ref[:, sl])
        ub = u.astype(BF16)
        ra = jnp.dot(ub, wa_ref[h], preferred_element_type=F32) + ba_ref[:, sl]
        ri = jnp.dot(ub, wi_ref[h], preferred_element_type=F32) + bi_ref[:, sl]
        log_a = RG_C * jax.nn.sigmoid(ra) * _log_sigmoid(lam_ref[:, sl])
        a = jnp.exp(log_a)
        b = jnp.sqrt(_neg_expm1(2.0 * log_a, a * a)) * (jax.nn.sigmoid(ri) * u)

        for sft in (1, 2, 4):
            if rev:
                a_s = pltpu.roll(a, r - sft, 0)
                b_s = pltpu.roll(b, r - sft, 0)
                m = sub < V7X_SUBLANES - sft
            else:
                a_s = pltpu.roll(a, sft, 0)
                b_s = pltpu.roll(b, sft, 0)
                m = sub >= sft
            b = jnp.where(m, a * b_s + b, b)
            a = jnp.where(m, a * a_s, a)

        carry = carry_ref[:, sl]
        outs = [None] * groups
        order = range(groups - 1, -1, -1) if rev else range(groups)
        for gi in order:
            lo = gi * V7X_SUBLANES
            hg = b[lo:lo + V7X_SUBLANES] + a[lo:lo + V7X_SUBLANES] * carry
            carry = hg[0:1] if rev else hg[V7X_SUBLANES - 1:V7X_SUBLANES]
            outs[gi] = hg
        carry_ref[:, sl] = carry
        out_ref[:, sl] = jnp.concatenate(outs, axis=0).astype(BF16)


def _rg2(lay, rec, conv_w, conv_b, w_a, w_i, b_a, b_i, lam, rev):
    n, width = rec.shape
    r, n_c, n_l = lay.R, lay.nC, lay.nL
    heads = w_a.shape[0]
    hd = width // heads
    hr = V7X_BF16_ROWS
    rb = r // hr
    last_hb = n // hr - 1

    def tile(b, s):
        if rev:
            c_t = b * n_c + (n_c - 1 - s)
            l_t = lay.nct + b * n_l + (n_l - 1 - (s - n_c))
        else:
            c_t = b * n_c + s
            l_t = lay.nct + b * n_l + (s - n_c)
        return jnp.where(s < n_c, c_t, l_t)

    const2 = lambda b, s: (0, 0)
    const3 = lambda b, s: (0, 0, 0)
    return pl.pallas_call(
        functools.partial(_rg2_kernel, rev=rev, n_c=n_c, n_l=n_l, r=r, hd=hd, heads=heads),
        grid=(lay.B, n_c + n_l),
        in_specs=[
            pl.BlockSpec((r, width), lambda b, s: (tile(b, s), 0)),
            pl.BlockSpec((hr, width), lambda b, s: (jnp.maximum(tile(b, s) * rb - 1, 0), 0)),
            pl.BlockSpec((hr, width), lambda b, s: (jnp.minimum((tile(b, s) + 1) * rb, last_hb), 0)),
            _resident((CONV_W, width), const2),
            _resident((1, width), const2),
            _resident((heads, hd, hd), const3),
            _resident((heads, hd, hd), const3),
            _resident((1, width), const2),
            _resident((1, width), const2),
            _resident((1, width), const2),
        ],
        out_specs=pl.BlockSpec((r, width), lambda b, s: (tile(b, s), 0)),
        out_shape=jax.ShapeDtypeStruct((n, width), BF16),
        scratch_shapes=[pltpu.VMEM((1, width), F32)],
        compiler_params=_params(("arbitrary", "arbitrary")),
        name="rg_scan_bwd" if rev else "rg_scan_fwd",
    )(rec, rec, rec, conv_w, conv_b, w_a, w_i, b_a, b_i, lam)


def _proj_res_kernel(*refs, mode, nct, cn):
    if mode == "gated":
        hf_ref, hb_ref, gate_ref, x_ref, g1_ref, w_ref, out_ref = refs
        a = ((hf_ref[...].astype(F32) + hb_ref[...].astype(F32)) * gate_ref[...].astype(F32)).astype(BF16)
    elif mode == "ctx_lat":
        ac_ref, al_ref, x_ref, g1_ref, w_ref, out_ref = refs
        a = jnp.where(pl.program_id(0) < nct, ac_ref[...], al_ref[...])
    else:
        a_ref, x_ref, g1_ref, w_ref, out_ref = refs
        a = a_ref[...]
    d = out_ref.shape[1]
    for c in range(d // cn):
        cs = slice(c * cn, (c + 1) * cn)
        y = jnp.dot(a, w_ref[:, cs], preferred_element_type=F32)
        out_ref[:, cs] = x_ref[:, cs] + g1_ref[:, cs] * y


def _proj_res(lay, a_list, x, mods, w_out, x_off=0):
    k, d = w_out.shape
    r = lay.R
    cn = _col_chunk(d)
    row = lambda i: (i, 0)
    mode = {1: "plain", 2: "ctx_lat", 3: "gated"}[len(a_list)]
    if mode == "ctx_lat":
        a_maps = [lambda i: (jnp.minimum(i, lay.nct - 1), 0), lambda i: (jnp.maximum(i - lay.nct, 0), 0)]
    else:
        a_maps = [row] * len(a_list)
    return pl.pallas_call(
        functools.partial(_proj_res_kernel, mode=mode, nct=lay.nct, cn=cn),
        grid=(lay.ntiles,),
        in_specs=[pl.BlockSpec((r, k), m) for m in a_maps] + [
            pl.BlockSpec((r, d), lambda i: (i + x_off, 0)),
            _mod_spec(lay, d, 2),
            _resident((k, d), lambda i: (0, 0)),
        ],
        out_specs=pl.BlockSpec((r, d), row),
        out_shape=jax.ShapeDtypeStruct((lay.N, d), F32),
        compiler_params=_params(("arbitrary",)),
        name="mixer_out_proj",
    )(*a_list, x, mods, w_out)


def _dft_tables(n):
    k = np.arange(n, dtype=np.int64)
    ang = 2.0 * np.pi * ((k[:, None] * k[None, :]) % n).astype(np.float64) / n
    s = 1.0 / np.sqrt(n)
    return (np.cos(ang) * s).astype(np.float32), (np.sin(ang) * s).astype(np.float32)


def _f1_kernel(x_ref, g_ref, sh_ref, sc_ref, cs_ref, zc_ref, zs_ref, *, groups, gd):
    h = _normmod(x_ref[...], g_ref[...], sh_ref[...], sc_ref[...]).astype(BF16)
    for gi in range(groups):
        sl = slice(gi * gd, (gi + 1) * gd)
        z = jnp.dot(h[:, sl], cs_ref[...], preferred_element_type=F32)
        zc_ref[:, sl] = z[:, :gd].astype(BF16)
        zs_ref[:, sl] = z[:, gd:].astype(BF16)


def _f1(lay, x, g, mods, cs, x_off=0):
    d = x.shape[1]
    r = lay.R
    gd = d // FN_GROUPS
    row = lambda i: (i, 0)
    return pl.pallas_call(
        functools.partial(_f1_kernel, groups=FN_GROUPS, gd=gd),
        grid=(lay.ntiles,),
        in_specs=[
            pl.BlockSpec((r, d), lambda i: (i + x_off, 0)),
            _resident((1, d), lambda i: (0, 0)),
            _mod_spec(lay, d, 0),
            _mod_spec(lay, d, 1),
            _resident((gd, 2 * gd), lambda i: (0, 0)),
        ],
        out_specs=[pl.BlockSpec((r, d), row), pl.BlockSpec((r, d), row)],
        out_shape=[jax.ShapeDtypeStruct((lay.N, d), BF16)] * 2,
        compiler_params=_params(("arbitrary",)),
        name="fourier_channel_dft",
    )(x, g, mods, mods, cs)


def _f2_kernel(ct_ref, st_ref, zc_ref, zs_ref, out_ref, *, t, rc):
    for c in range(t // rc):
        rs = slice(c * rc, (c + 1) * rc)
        y = jnp.dot(ct_ref[rs, :], zc_ref[...], preferred_element_type=F32)
        y = y - jnp.dot(st_ref[rs, :], zs_ref[...], preferred_element_type=F32)
        out_ref[rs, :] = y.astype(BF16)


def _f2(zc, zs, ct, st, batch, t, blk_off):
    d = zc.shape[1]
    tn = _col_chunk(d)
    rc = _col_chunk(t)
    zmap = lambda b, j: (blk_off + b, j)
    return pl.pallas_call(
        functools.partial(_f2_kernel, t=t, rc=rc),
        grid=(batch, d // tn),
        in_specs=[
            _resident((t, t), lambda b, j: (0, 0)),
            _resident((t, t), lambda b, j: (0, 0)),
            pl.BlockSpec((t, tn), zmap),
            pl.BlockSpec((t, tn), zmap),
        ],
        out_specs=pl.BlockSpec((t, tn), lambda b, j: (b, j)),
        out_shape=jax.ShapeDtypeStruct((batch * t, d), BF16),
        compiler_params=_params(("arbitrary", "arbitrary")),
        name="fourier_time_dft",
    )(ct, st, zc, zs)


def _pack_pairs(v):
    half = v.shape[1] // 2
    bits = pltpu.bitcast(v.astype(BF16).astype(F32), jnp.uint32)
    return (bits[:, :half] >> 16) | (bits[:, half:] & jnp.uint32(0xFFFF0000))


def _unpack_pairs(w):
    lo = pltpu.bitcast(w << 16, F32)
    hi = pltpu.bitcast(w & jnp.uint32(0xFFFF0000), F32)
    return lo, hi


def _router_kernel(x_ref, g_ref, sh_ref, sc_ref, rw_ref, rb_ref, tri_e_ref, tri_t_ref,
                   f_ref, ek_ref, rk_ref, wk_ref, cnt_ref, base_ref, *, n_exp):
    @pl.when(pl.program_id(0) == 0)
    def _():
        base_ref[...] = jnp.zeros_like(base_ref)

    f = _normmod(x_ref[...], g_ref[...], sh_ref[...], sc_ref[...])
    f_hi = f.astype(BF16)
    f_lo = (f - f_hi.astype(F32)).astype(BF16)
    f_ref[...] = _pack_pairs(f)
    rw = rw_ref[...]
    rw_hi = rw.astype(BF16)
    rw_lo = (rw - rw_hi.astype(F32)).astype(BF16)
    logits = _nt_dot(rw_hi, f_hi) + (_nt_dot(rw_hi, f_lo) + _nt_dot(rw_lo, f_hi))
    scores = jax.nn.sigmoid(logits)
    biased = scores + rb_ref[...]
    r = scores.shape[1]
    gsz = n_exp // N_GROUPS
    sub = lax.broadcasted_iota(jnp.int32, (gsz, r), 0)
    neg = jnp.float32(-jnp.inf)

    blocks = [biased[gi * gsz:(gi + 1) * gsz] for gi in range(N_GROUPS)]
    gscore = []
    for blk in blocks:
        m1 = jnp.max(blk, axis=0, keepdims=True)
        first = jnp.min(jnp.where(blk == m1, sub, gsz), axis=0, keepdims=True)
        m2 = jnp.max(jnp.where(sub == first, neg, blk), axis=0, keepdims=True)
        gscore.append(m1 + m2)
    masked = []
    for gi in range(N_GROUPS):
        rank = jnp.zeros((1, r), jnp.int32)
        for gj in range(N_GROUPS):
            if gj == gi:
                continue
            ahead = (gscore[gj] >= gscore[gi]) if gj < gi else (gscore[gj] > gscore[gi])
            rank = rank + ahead.astype(jnp.int32)
        masked.append(jnp.where(rank < TOPK_GROUPS, blocks[gi], neg))
    sel_f, sel_w = [], []
    for gi in range(N_GROUPS):
        blk = masked[gi]
        rank = jnp.zeros((gsz, r), jnp.int32)
        for gj in range(N_GROUPS):
            for j in range(gsz):
                other = masked[gj][j:j + 1]
                if gj < gi:
                    ahead = other >= blk
                elif gj > gi:
                    ahead = other > blk
                else:
                    ahead = (other > blk) | ((other == blk) & (sub > j))
                rank = rank + ahead.astype(jnp.int32)
        sel_f.append(jnp.where(rank < TOP_K, 1.0, 0.0))
        sel_w.append(jnp.where(rank < TOP_K, scores[gi * gsz:(gi + 1) * gsz], 0.0))
    tot = sel_w[0]
    for gi in range(1, N_GROUPS):
        tot = tot + sel_w[gi]
    denom = jnp.sum(tot, axis=0, keepdims=True)
    gates = jnp.concatenate(sel_w, axis=0) / denom * ROUTED_SCALE
    sel = jnp.concatenate(sel_f, axis=0)
    sel_b = sel.astype(BF16)
    slot = jnp.dot(tri_e_ref[...], sel_b, preferred_element_type=F32)
    base = base_ref[...]
    pos_in_expert = base + jnp.dot(sel_b, tri_t_ref[...], preferred_element_type=F32)
    base = base + jnp.sum(sel, axis=1, keepdims=True)
    base_ref[...] = base
    cnt_ref[...] = base
    e_idx = lax.broadcasted_iota(jnp.int32, sel.shape, 0).astype(F32)
    for k in range(TOP_K):
        hit = (sel > 0.0) & (slot == float(k))
        ek_ref[k:k + 1, :] = jnp.sum(jnp.where(hit, e_idx, 0.0), axis=0, keepdims=True).astype(jnp.int32)
        rk_ref[k:k + 1, :] = jnp.sum(jnp.where(hit, pos_in_expert, 0.0), axis=0, keepdims=True).astype(jnp.int32)
        wk_ref[k:k + 1, :] = jnp.sum(jnp.where(hit, gates, 0.0), axis=0, keepdims=True)


def _router(lay, x, g, mods, rw_t, rb):
    d = x.shape[1]
    n_exp = rw_t.shape[0]
    r = lay.R
    row = lambda i: (i, 0)
    tok = lambda i: (0, i)
    const = lambda i: (0, 0)
    tri_e = jnp.asarray(np.tril(np.ones((n_exp, n_exp), np.float32), -1)).astype(BF16)
    tri_t = jnp.asarray(np.triu(np.ones((r, r), np.float32), 1)).astype(BF16)
    return pl.pallas_call(
        functools.partial(_router_kernel, n_exp=n_exp),
        grid=(lay.ntiles,),
        in_specs=[
            pl.BlockSpec((r, d), row),
            _resident((1, d), const),
            _mod_spec(lay, d, 3),
            _mod_spec(lay, d, 4),
            _resident((n_exp, d), const),
            _resident((n_exp, 1), const),
            _resident((n_exp, n_exp), const),
            _resident((r, r), const),
        ],
        out_specs=[
            pl.BlockSpec((r, d // 2), row),
            pl.BlockSpec((TOP_K, r), tok),
            pl.BlockSpec((TOP_K, r), tok),
            pl.BlockSpec((TOP_K, r), tok),
            pl.BlockSpec((n_exp, 1), const),
        ],
        out_shape=[
            jax.ShapeDtypeStruct((lay.N, d // 2), jnp.uint32),
            jax.ShapeDtypeStruct((TOP_K, lay.N), jnp.int32),
            jax.ShapeDtypeStruct((TOP_K, lay.N), jnp.int32),
            jax.ShapeDtypeStruct((TOP_K, lay.N), F32),
            jax.ShapeDtypeStruct((n_exp, 1), F32),
        ],
        scratch_shapes=[pltpu.VMEM((n_exp, 1), F32)],
        compiler_params=_params(("arbitrary",)),
        name="moe_router",
    )(x, g, mods, mods, rw_t, rb, tri_e, tri_t)


def _dispatch_kernel(pad_start_ref, pad_len_ref, nused_ref, p_ref, f_ref, xs_ref, sem, zeros_ref,
                     *, n_exp, r, tmx):
    i = pl.program_id(0)

    def scatter_rows(t, carry):
        for k in range(TOP_K):
            p = p_ref[t * TOP_K + k]
            pltpu.make_async_copy(f_ref.at[pl.ds(t, 1)], xs_ref.at[pl.ds(p, 1)], sem.at[0]).start()
        return carry

    lax.fori_loop(0, r, scatter_rows, 0)
    pltpu.make_async_copy(xs_ref.at[pl.ds(0, r * TOP_K)], xs_ref.at[pl.ds(0, r * TOP_K)], sem.at[0]).wait()

    @pl.when(i == pl.num_programs(0) - 1)
    def _():
        zeros_ref[...] = jnp.zeros_like(zeros_ref)

        def zero_pad(e, carry):
            start = pad_start_ref[e]

            def zero_row(j, c):
                cp = pltpu.make_async_copy(zeros_ref.at[pl.ds(0, 1)], xs_ref.at[pl.ds(start + j, 1)], sem.at[1])
                cp.start()
                cp.wait()
                return c

            return lax.fori_loop(0, pad_len_ref[e], zero_row, carry)

        lax.fori_loop(0, n_exp, zero_pad, 0)

        def zero_tile(m, c):
            dst = xs_ref.at[pl.ds(pl.multiple_of(m * tmx, tmx), tmx)]
            cp = pltpu.make_async_copy(zeros_ref, dst, sem.at[1])
            cp.start()
            cp.wait()
            return c

        lax.fori_loop(nused_ref[0], xs_ref.shape[0] // tmx, zero_tile, 0)


def _dispatch(lay, fpk, p_flat, pad_start, pad_len, n_used, n_rows, tmx):
    n, half = fpk.shape
    r = lay.R
    n_exp = pad_start.shape[0]
    grid_spec = pltpu.PrefetchScalarGridSpec(
        num_scalar_prefetch=3,
        grid=(lay.ntiles,),
        in_specs=[
            pl.BlockSpec((r * TOP_K,), lambda i, *_: (i,), memory_space=pltpu.SMEM),
            pl.BlockSpec((r, half), lambda i, *_: (i, 0)),
        ],
        out_specs=pl.BlockSpec(memory_space=pl.ANY),
        scratch_shapes=[pltpu.SemaphoreType.DMA((2,)), pltpu.VMEM((tmx, half), jnp.uint32)],
    )
    return pl.pallas_call(
        functools.partial(_dispatch_kernel, n_exp=n_exp, r=r, tmx=tmx),
        grid_spec=grid_spec,
        out_shape=jax.ShapeDtypeStruct((n_rows, half), jnp.uint32),
        compiler_params=_params(("arbitrary",)),
        name="moe_dispatch",
    )(pad_start, pad_len, n_used, p_flat, fpk)


def _swiglu_packed(xw, wgu_ref, wd_ref, fdim):
    half = xw.shape[1]
    lo, hi = _unpack_pairs(xw)
    hgu = jnp.dot(lo.astype(BF16), wgu_ref[:half, :], preferred_element_type=F32)
    hgu = hgu + jnp.dot(hi.astype(BF16), wgu_ref[half:, :], preferred_element_type=F32)
    hid = (jax.nn.silu(hgu[:, :fdim]) * hgu[:, fdim:]).astype(BF16)
    return jnp.dot(hid, wd_ref[...], preferred_element_type=F32)


def _grouped_kernel(te_ref, nused_ref, xs_ref, wgu_ref, wd_ref, ys_ref, *, fdim):
    m = pl.program_id(0)

    @pl.when(m < nused_ref[0])
    def _():
        ys_ref[...] = _pack_pairs(_swiglu_packed(xs_ref[...], wgu_ref, wd_ref, fdim))

    @pl.when(m >= nused_ref[0])
    def _():
        ys_ref[...] = jnp.zeros_like(ys_ref)


def _grouped(xs, wgu, wd, tile_expert, n_used, tmx):
    n_rows, half = xs.shape
    _, d, f2 = wgu.shape
    fdim = f2 // 2
    grid_spec = pltpu.PrefetchScalarGridSpec(
        num_scalar_prefetch=2,
        grid=(n_rows // tmx,),
        in_specs=[
            pl.BlockSpec((tmx, half), lambda m, te, nu: (jnp.minimum(m, nu[0] - 1), 0)),
            pl.BlockSpec((None, d, f2), lambda m, te, nu: (te[m], 0, 0)),
            pl.BlockSpec((None, fdim, d), lambda m, te, nu: (te[m], 0, 0)),
        ],
        out_specs=pl.BlockSpec((tmx, half), lambda m, te, nu: (m, 0)),
    )
    return pl.pallas_call(
        functools.partial(_grouped_kernel, fdim=fdim),
        grid_spec=grid_spec,
        out_shape=jax.ShapeDtypeStruct((n_rows, half), jnp.uint32),
        compiler_params=_params(("arbitrary",)),
        name="moe_experts",
    )(tile_expert, n_used, xs, wgu, wd)


def _combine_kernel(p_ref, w_ref, f_ref, x_ref, g2_ref, swgu_ref, swd_ref, ys_ref, out_ref, buf_ref, sem,
                    *, r, fdim):
    def gather_rows(t, carry):
        for k in range(TOP_K):
            p = p_ref[t * TOP_K + k]
            pltpu.make_async_copy(ys_ref.at[pl.ds(p, 1)], buf_ref.at[k, pl.ds(t, 1)], sem.at[0]).start()
        return carry

    lax.fori_loop(0, r, gather_rows, 0)
    shared = _swiglu_packed(f_ref[...], swgu_ref, swd_ref, fdim)
    pltpu.make_async_copy(buf_ref, buf_ref, sem.at[0]).wait()

    half = f_ref.shape[1]
    acc_lo = shared[:, :half]
    acc_hi = shared[:, half:]
    w = w_ref[...]
    for k in range(TOP_K):
        lo, hi = _unpack_pairs(buf_ref[k])
        wcol = w[:, k:k + 1]
        acc_lo = acc_lo + wcol * lo
        acc_hi = acc_hi + wcol * hi
    out_ref[:, :half] = x_ref[:, :half] + g2_ref[:, :half] * acc_lo
    out_ref[:, half:] = x_ref[:, half:] + g2_ref[:, half:] * acc_hi


def _combine(lay, ys, p_flat, w_tok, fpk, x, mods, swgu, swd):
    n, d = x.shape
    half = d // 2
    r = lay.R
    fdim = swd.shape[0]
    row = lambda i: (i, 0)
    const = lambda i: (0, 0)
    return pl.pallas_call(
        functools.partial(_combine_kernel, r=r, fdim=fdim),
        grid=(lay.ntiles,),
        in_specs=[
            pl.BlockSpec((r * TOP_K,), lambda i: (i,), memory_space=pltpu.SMEM),
            pl.BlockSpec((r, TOP_K), row),
            pl.BlockSpec((r, half), row),
            pl.BlockSpec((r, d), row),
            _mod_spec(lay, d, 5),
            _resident((d, 2 * fdim), const),
            _resident((fdim, d), const),
            pl.BlockSpec(memory_space=pl.ANY),
        ],
        out_specs=pl.BlockSpec((r, d), row),
        out_shape=jax.ShapeDtypeStruct((n, d), F32),
        scratch_shapes=[pltpu.VMEM((TOP_K, r, half), jnp.uint32), pltpu.SemaphoreType.DMA((1,))],
        compiler_params=_params(("arbitrary",)),
        name="moe_combine",
    )(p_flat, w_tok, fpk, x, mods, swgu, swd, ys)


def _moe(lay, x, g2n, mods, rw_t, rb, wgu, wd, swgu, swd):
    n_exp = rw_t.shape[0]
    tmx = 2 * lay.R
    fpk, ek, rk, wk, cnt = _router(lay, x, g2n, mods, rw_t, rb)
    cnt = cnt[:, 0].astype(jnp.int32)
    padded = (cnt + tmx - 1) // tmx * tmx
    offs = jnp.concatenate([jnp.zeros((1,), jnp.int32), jnp.cumsum(padded)])
    n_tiles = lay.N * TOP_K // tmx + n_exp
    p_flat = (jnp.take(offs, ek) + rk).T.reshape(-1)
    w_tok = wk.T
    n_used = (offs[n_exp] // tmx).reshape(1)
    tile_start = jnp.minimum(jnp.arange(n_tiles, dtype=jnp.int32), n_used[0] - 1) * tmx
    tile_expert = jnp.sum((tile_start[:, None] >= offs[None, 1:]).astype(jnp.int32), axis=1)
    xs = _dispatch(lay, fpk, p_flat, offs[:-1] + cnt, padded - cnt, n_used, n_tiles * tmx, tmx)
    ys = _grouped(xs, wgu, wd, tile_expert, n_used, tmx)
    return _combine(lay, ys, p_flat, w_tok, fpk, x, mods, swgu, swd)


def _final_norm_kernel(x_ref, g_ref, out_ref):
    x = x_ref[...]
    ms = jnp.mean(x * x, axis=-1, keepdims=True)
    out_ref[...] = x * lax.rsqrt(ms + EPS) * g_ref[...]


def _final_norm(x, g, r):
    n, d = x.shape
    return pl.pallas_call(
        _final_norm_kernel,
        grid=(n // r,),
        in_specs=[pl.BlockSpec((r, d), lambda i: (i, 0)), _resident((1, d), lambda i: (0, 0))],
        out_specs=pl.BlockSpec((r, d), lambda i: (i, 0)),
        out_shape=jax.ShapeDtypeStruct((n, d), F32),
        compiler_params=_params(("arbitrary",)),
        name="final_norm",
    )(x, g)


def _grid_pos_embed(rows, dim):
    row = np.repeat(np.arange(rows, dtype=np.float32), GRID_W)
    col = np.tile(np.arange(GRID_W, dtype=np.float32), rows)
    quarter = dim // 4
    freqs = jnp.float32(POS_THETA) ** (-jnp.arange(quarter, dtype=F32) / quarter)
    ar = jnp.asarray(row)[:, None] * freqs
    ac = jnp.asarray(col)[:, None] * freqs
    return jnp.concatenate([jnp.sin(ar), jnp.cos(ar), jnp.sin(ac), jnp.cos(ac)], axis=-1).astype(F32)


def kernel(x, c, ctx, c_ctx, norm1_g, norm2_g, final_norm_g, w_mod, b_mod, rg_w_in, rg_conv_w, rg_conv_b, rg_w_a, rg_b_a, rg_w_i, rg_b_i, rg_lam, rg_w_out, fn_w_out, router_w, router_b, exp_w_gate, exp_w_up, exp_w_down, sh_w_gate, sh_w_up, sh_w_down):
    batch, t_lat, d = x.shape
    t_ctx = ctx.shape[1]
    depth = w_mod.shape[0]
    lay = _Layout(batch, t_ctx, t_lat, with_ctx=True)
    lay_lat = _Layout(batch, t_ctx, t_lat, with_ctx=False)

    n_cond = 1 + batch
    pad = (-n_cond) % V7X_SUBLANES
    cond = jnp.concatenate([c_ctx[None], c, jnp.zeros((pad, d), F32)], axis=0)
    mods_all = _ada(cond, w_mod, b_mod)

    pos = _grid_pos_embed(t_lat // GRID_W, d)
    xall = jnp.concatenate([ctx.reshape(batch * t_ctx, d), x.reshape(batch * t_lat, d)], axis=0)
    xall = _addpos(lay, xall, pos)

    gd = d // FN_GROUPS
    cg, sg = _dft_tables(gd)
    cs_tab = jnp.asarray(np.concatenate([cg, sg], axis=1)).astype(BF16)
    ct_lat, st_lat = (jnp.asarray(a).astype(BF16) for a in _dft_tables(t_lat))
    ct_ctx, st_ctx = (jnp.asarray(a).astype(BF16) for a in _dft_tables(t_ctx))

    for i in range(depth):
        last = i == depth - 1
        j = i // N_MIXERS
        mods = mods_all[i].reshape(mods_all.shape[1], 1, N_MOD * d)
        g1n = norm1_g[i][None]
        g2n = norm2_g[i][None]
        if i % N_MIXERS == 0:
            cur = lay
            gate, rec = _rg1(cur, xall, g1n, mods, rg_w_in[j].astype(BF16))
            hs = []
            for dr in range(2):
                hs.append(_rg2(cur, rec, rg_conv_w[j], rg_conv_b[j][None],
                               rg_w_a[j, dr].astype(BF16), rg_w_i[j, dr].astype(BF16),
                               rg_b_a[j, dr][None], rg_b_i[j, dr][None], rg_lam[j, dr][None],
                               rev=dr == 1))
            xall = _proj_res(cur, [hs[0], hs[1], gate], xall, mods, rg_w_out[j].astype(BF16))
        else:
            cur = lay_lat if last else lay
            x_off = lay.nct if last else 0
            zc, zs = _f1(cur, xall, g1n, mods, cs_tab, x_off=x_off)
            fre = [_f2(zc, zs, ct_lat, st_lat, batch, t_lat, cur.nct * cur.R // t_lat)]
            if not last:
                fre = [_f2(zc, zs, ct_ctx, st_ctx, batch, t_ctx, 0)] + fre
            xall = _proj_res(cur, fre, xall, mods, fn_w_out[j].astype(BF16), x_off=x_off)
        if last and cur is lay:
            xall = xall[lay.nct * lay.R:]
            cur = lay_lat
        wgu = jnp.concatenate([exp_w_gate[i], exp_w_up[i]], axis=-1).astype(BF16)
        wd = exp_w_down[i].astype(BF16)
        swgu = jnp.concatenate([sh_w_gate[i], sh_w_up[i]], axis=-1).astype(BF16)
        swd = sh_w_down[i].astype(BF16)
        xall = _moe(cur, xall, g2n, mods, router_w[i].T, router_b[i][:, None], wgu, wd, swgu, swd)

    out = _final_norm(xall, final_norm_g[None], lay_lat.R)
    return out.reshape(batch, t_lat, d)
```

```python
import functools
import math

import numpy as np
import jax
import jax.numpy as jnp
from jax import lax
from jax.experimental import pallas as pl
from jax.experimental.pallas import tpu as pltpu

GRID_W = 64
N_MIXERS = 2
LRU_HEADS = 8
CONV_W = 4
CONV_LEFT = 2
RG_C = 8.0
FN_GROUPS = 8
TOP_K = 8
N_GROUPS = 8
TOPK_GROUPS = 4
ROUTED_SCALE = 2.5
N_MOD = 6
EPS = 1e-6
POS_THETA = 10000.0

V7X_VMEM_BYTES = 64 * 1024 * 1024
V7X_SUBLANES = 8
V7X_BF16_ROWS = 16
V7X_MXU_DIM = 256
MAX_ROW_TILE = 256
VMEM_LIMIT = 56 * 1024 * 1024

F32 = jnp.float32
BF16 = jnp.bfloat16


def _params(sem, vmem=VMEM_LIMIT):
    return pltpu.CompilerParams(dimension_semantics=sem, vmem_limit_bytes=vmem)


def _resident(shape, index_map):
    return pl.BlockSpec(shape, index_map, pipeline_mode=pl.Buffered(1))


def _col_chunk(n, cap=512):
    c = min(n, cap)
    while n % c:
        c //= 2
    return c


def _normmod(x, g, shift, scale):
    ms = jnp.mean(x * x, axis=-1, keepdims=True)
    y = x * lax.rsqrt(ms + EPS) * g
    return y * (1.0 + scale) + shift


def _nt_dot(a, b):
    return lax.dot_general(a, b, (((1,), (1,)), ((), ())), preferred_element_type=F32)


class _Layout:
    def __init__(self, batch, t_ctx, t_lat, with_ctx=True):
        g = math.gcd(t_ctx, t_lat)
        r = min(g, MAX_ROW_TILE)
        while g % r or r % V7X_BF16_ROWS:
            r -= 1
        assert r >= V7X_BF16_ROWS
        self.R = r
        self.B = batch
        self.Tc = t_ctx
        self.T = t_lat
        self.nC = t_ctx // r if with_ctx else 0
        self.nL = t_lat // r
        self.nct = batch * self.nC
        self.ntiles = self.nct + batch * self.nL
        self.N = self.ntiles * r
        assert (batch * t_ctx) % t_lat == 0 or not with_ctx

    def seg(self, i, tile=None):
        tile = self.R if tile is None else tile
        nct = self.nct * self.R // tile
        n_l = self.T // tile
        return jnp.where(i < nct, 0, 1 + (i - nct) // n_l)


def _mod_spec(lay, d, which, tile=None):
    return pl.BlockSpec((None, 1, d), lambda i, *_: (lay.seg(i, tile), 0, which))


def _ada_kernel(cond_ref, w_ref, b_ref, out_ref):
    s = jax.nn.silu(cond_ref[...]).astype(BF16)
    out_ref[...] = jnp.dot(s, w_ref[...].astype(BF16), preferred_element_type=F32) + b_ref[...]


def _ada(cond, w_mod, b_mod):
    depth, d, nm = w_mod.shape
    rows = cond.shape[0]
    tn = _col_chunk(nm, 1024)
    return pl.pallas_call(
        _ada_kernel,
        grid=(depth, nm // tn),
        in_specs=[
            pl.BlockSpec((rows, d), lambda l, j: (0, 0)),
            pl.BlockSpec((None, d, tn), lambda l, j: (l, 0, j)),
            pl.BlockSpec((None, 1, tn), lambda l, j: (l, 0, j)),
        ],
        out_specs=pl.BlockSpec((None, rows, tn), lambda l, j: (l, 0, j)),
        out_shape=jax.ShapeDtypeStruct((depth, rows, nm), F32),
        compiler_params=_params(("arbitrary", "arbitrary")),
        name="ada_mod",
    )(cond, w_mod, b_mod.reshape(depth, 1, nm))


def _addpos_kernel(x_ref, pos_ref, out_ref, *, nct):
    i = pl.program_id(0)
    scale = jnp.where(i >= nct, 1.0, 0.0).astype(F32)
    out_ref[...] = x_ref[...] + pos_ref[...] * scale


def _addpos(lay, xall, pos):
    d = xall.shape[1]
    r = lay.R
    return pl.pallas_call(
        functools.partial(_addpos_kernel, nct=lay.nct),
        grid=(lay.ntiles,),
        in_specs=[
            pl.BlockSpec((r, d), lambda i: (i, 0)),
            pl.BlockSpec((r, d), lambda i: (jnp.where(i < lay.nct, 0, (i - lay.nct) % lay.nL), 0)),
        ],
        out_specs=pl.BlockSpec((r, d), lambda i: (i, 0)),
        out_shape=jax.ShapeDtypeStruct(xall.shape, F32),
        compiler_params=_params(("arbitrary",)),
        name="add_pos",
    )(xall, pos)


def _rg1_kernel(x_ref, g_ref, sh_ref, sc_ref, w_ref, gate_ref, rec_ref, *, width, cn):
    h = _normmod(x_ref[...], g_ref[...], sh_ref[...], sc_ref[...]).astype(BF16)
    for c in range(2 * width // cn):
        z = jnp.dot(h, w_ref[:, c * cn:(c + 1) * cn], preferred_element_type=F32)
        if c * cn < width:
            gate_ref[:, c * cn:(c + 1) * cn] = jax.nn.gelu(z).astype(BF16)
        else:
            rec_ref[:, c * cn - width:(c + 1) * cn - width] = z.astype(BF16)


def _rg1(lay, x, g, mods, w_in):
    d = x.shape[1]
    width = w_in.shape[1] // 2
    r = lay.R
    cn = _col_chunk(width)
    row = lambda i: (i, 0)
    return pl.pallas_call(
        functools.partial(_rg1_kernel, width=width, cn=cn),
        grid=(lay.ntiles,),
        in_specs=[
            pl.BlockSpec((r, d), row),
            _resident((1, d), lambda i: (0, 0)),
            _mod_spec(lay, d, 0),
            _mod_spec(lay, d, 1),
            _resident((d, 2 * width), lambda i: (0, 0)),
        ],
        out_specs=[pl.BlockSpec((r, width), row), pl.BlockSpec((r, width), row)],
        out_shape=[jax.ShapeDtypeStruct((lay.N, width), BF16)] * 2,
        compiler_params=_params(("arbitrary",)),
        name="rg_in_proj",
    )(x, g, mods, mods, w_in)


def _log_sigmoid(x):
    return jnp.minimum(x, 0.0) - jnp.log1p(jnp.exp(-jnp.abs(x)))


def _neg_expm1(t, exp_t):
    series = t * (1.0 + t * (0.5 + t * (1.0 / 6.0 + t * (1.0 / 24.0 + t * (1.0 / 120.0)))))
    return -jnp.where(t > -0.1, series, exp_t - 1.0)


def _rg2_kernel(cur_ref, prev_ref, next_ref, cw_ref, cb_ref, wa_ref, wi_ref, ba_ref, bi_ref,
                lam_ref, out_ref, carry_ref, *, rev, n_c, n_l, r, hd, heads):
    s = pl.program_id(1)

    @pl.when(s == 0)
    def _():
        carry_ref[...] = jnp.zeros_like(carry_ref)

    in_ctx = s < n_c
    pos = jnp.where(in_ctx, s, s - n_c)
    n_seq = jnp.where(in_ctx, n_c, n_l)
    chunk = (n_seq - 1 - pos) if rev else pos
    keep_prev = jnp.where(chunk == 0, 0.0, 1.0).astype(F32)
    keep_next = jnp.where(chunk == n_seq - 1, 0.0, 1.0).astype(F32)

    row = lax.broadcasted_iota(jnp.int32, (r, hd), 0)
    sub = row % V7X_SUBLANES
    hrows = V7X_BF16_ROWS
    groups = r // V7X_SUBLANES

    for h in range(heads):
        sl = slice(h * hd, (h + 1) * hd)
        cur = cur_ref[:, sl].astype(F32)
        pv = prev_ref[:, sl].astype(F32) * keep_prev
        nx = next_ref[:, sl].astype(F32) * keep_next
        p_m2 = pv[hrows - 2:hrows - 1, :]
        p_m1 = pv[hrows - 1:hrows, :]
        n_p1 = nx[0:1, :]
        x_m1 = jnp.where(row == 0, p_m1, pltpu.roll(cur, 1, 0))
        x_m2 = jnp.where(row == 0, p_m2, jnp.where(row == 1, p_m1, pltpu.roll(cur, 2, 0)))
        x_p1 = jnp.where(row == r - 1, n_p1, pltpu.roll(cur, r - 1, 0))
        u = (cw_ref[0:1, sl] * x_m2 + cw_ref[1:2, sl] * x_m1 + cw_ref[2:3, sl] * cur
             + cw_ref[3:4, sl] * x_p1 + cb_ref[:, sl])
        ub = u.astype(BF16)
        ra = jnp.dot(ub, wa_ref[h], preferred_element_type=F32) + ba_ref[:, sl]
        ri = jnp.dot(ub, wi_ref[h], preferred_element_type=F32) + bi_ref[:, sl]
        log_a = RG_C * jax.nn.sigmoid(ra) * _log_sigmoid(lam_ref[:, sl])
        a = jnp.exp(log_a)
        b = jnp.sqrt(_neg_expm1(2.0 * log_a, a * a)) * (jax.nn.sigmoid(ri) * u)

        for sft in (1, 2, 4):
            if rev:
                a_s = pltpu.roll(a, r - sft, 0)
                b_s = pltpu.roll(b, r - sft, 0)
                m = sub < V7X_SUBLANES - sft
            else:
                a_s = pltpu.roll(a, sft, 0)
                b_s = pltpu.roll(b, sft, 0)
                m = sub >= sft
            b = jnp.where(m, a * b_s + b, b)
            a = jnp.where(m, a * a_s, a)

        carry = carry_ref[:, sl]
        outs = [None] * groups
        order = range(groups - 1, -1, -1) if rev else range(groups)
        for gi in order:
            lo = gi * V7X_SUBLANES
            hg = b[lo:lo + V7X_SUBLANES] + a[lo:lo + V7X_SUBLANES] * carry
            carry = hg[0:1] if rev else hg[V7X_SUBLANES - 1:V7X_SUBLANES]
            outs[gi] = hg
        carry_ref[:, sl] = carry
        out_ref[:, sl] = jnp.concatenate(outs, axis=0).astype(BF16)


def _rg2(lay, rec, conv_w, conv_b, w_a, w_i, b_a, b_i, lam, rev):
    n, width = rec.shape
    r, n_c, n_l = lay.R, lay.nC, lay.nL
    heads = w_a.shape[0]
    hd = width // heads
    hr = V7X_BF16_ROWS
    rb = r // hr
    last_hb = n // hr - 1

    def tile(b, s):
        if rev:
            c_t = b * n_c + (n_c - 1 - s)
            l_t = lay.nct + b * n_l + (n_l - 1 - (s - n_c))
        else:
            c_t = b * n_c + s
            l_t = lay.nct + b * n_l + (s - n_c)
        return jnp.where(s < n_c, c_t, l_t)

    const2 = lambda b, s: (0, 0)
    const3 = lambda b, s: (0, 0, 0)
    return pl.pallas_call(
        functools.partial(_rg2_kernel, rev=rev, n_c=n_c, n_l=n_l, r=r, hd=hd, heads=heads),
        grid=(lay.B, n_c + n_l),
        in_specs=[
            pl.BlockSpec((r, width), lambda b, s: (tile(b, s), 0)),
            pl.BlockSpec((hr, width), lambda b, s: (jnp.maximum(tile(b, s) * rb - 1, 0), 0)),
            pl.BlockSpec((hr, width), lambda b, s: (jnp.minimum((tile(b, s) + 1) * rb, last_hb), 0)),
            _resident((CONV_W, width), const2),
            _resident((1, width), const2),
            _resident((heads, hd, hd), const3),
            _resident((heads, hd, hd), const3),
            _resident((1, width), const2),
            _resident((1, width), const2),
            _resident((1, width), const2),
        ],
        out_specs=pl.BlockSpec((r, width), lambda b, s: (tile(b, s), 0)),
        out_shape=jax.ShapeDtypeStruct((n, width), BF16),
        scratch_shapes=[pltpu.VMEM((1, width), F32)],
        compiler_params=_params(("arbitrary", "arbitrary")),
        name="rg_scan_bwd" if rev else "rg_scan_fwd",
    )(rec, rec, rec, conv_w, conv_b, w_a, w_i, b_a, b_i, lam)


def _proj_res_kernel(*refs, mode, nct, cn):
    if mode == "gated":
        hf_ref, hb_ref, gate_ref, x_ref, g1_ref, w_ref, out_ref = refs
        a = ((hf_ref[...].astype(F32) + hb_ref[...].astype(F32)) * gate_ref[...].astype(F32)).astype(BF16)
    elif mode == "ctx_lat":
        ac_ref, al_ref, x_ref, g1_ref, w_ref, out_ref = refs
        a = jnp.where(pl.program_id(0) < nct, ac_ref[...], al_ref[...])
    else:
        a_ref, x_ref, g1_ref, w_ref, out_ref = refs
        a = a_ref[...]
    d = out_ref.shape[1]
    for c in range(d // cn):
        cs = slice(c * cn, (c + 1) * cn)
        y = jnp.dot(a, w_ref[:, cs], preferred_element_type=F32)
        out_ref[:, cs] = x_ref[:, cs] + g1_ref[:, cs] * y


def _proj_res(lay, a_list, x, mods, w_out, x_off=0):
    k, d = w_out.shape
    r = lay.R
    cn = _col_chunk(d)
    row = lambda i: (i, 0)
    mode = {1: "plain", 2: "ctx_lat", 3: "gated"}[len(a_list)]
    if mode == "ctx_lat":
        a_maps = [lambda i: (jnp.minimum(i, lay.nct - 1), 0), lambda i: (jnp.maximum(i - lay.nct, 0), 0)]
    else:
        a_maps = [row] * len(a_list)
    return pl.pallas_call(
        functools.partial(_proj_res_kernel, mode=mode, nct=lay.nct, cn=cn),
        grid=(lay.ntiles,),
        in_specs=[pl.BlockSpec((r, k), m) for m in a_maps] + [
            pl.BlockSpec((r, d), lambda i: (i + x_off, 0)),
            _mod_spec(lay, d, 2),
            _resident((k, d), lambda i: (0, 0)),
        ],
        out_specs=pl.BlockSpec((r, d), row),
        out_shape=jax.ShapeDtypeStruct((lay.N, d), F32),
        compiler_params=_params(("arbitrary",)),
        name="mixer_out_proj",
    )(*a_list, x, mods, w_out)


def _dft_tables(n):
    k = np.arange(n, dtype=np.int64)
    ang = 2.0 * np.pi * ((k[:, None] * k[None, :]) % n).astype(np.float64) / n
    s = 1.0 / np.sqrt(n)
    return (np.cos(ang) * s).astype(np.float32), (np.sin(ang) * s).astype(np.float32)


def _f1_kernel(x_ref, g_ref, sh_ref, sc_ref, cs_ref, zc_ref, zs_ref, *, groups, gd):
    h = _normmod(x_ref[...], g_ref[...], sh_ref[...], sc_ref[...]).astype(BF16)
    for gi in range(groups):
        sl = slice(gi * gd, (gi + 1) * gd)
        z = jnp.dot(h[:, sl], cs_ref[...], preferred_element_type=F32)
        zc_ref[:, sl] = z[:, :gd].astype(BF16)
        zs_ref[:, sl] = z[:, gd:].astype(BF16)


def _f1(lay, x, g, mods, cs, x_off=0):
    d = x.shape[1]
    r = lay.R
    gd = d // FN_GROUPS
    row = lambda i: (i, 0)
    return pl.pallas_call(
        functools.partial(_f1_kernel, groups=FN_GROUPS, gd=gd),
        grid=(lay.ntiles,),
        in_specs=[
            pl.BlockSpec((r, d), lambda i: (i + x_off, 0)),
            _resident((1, d), lambda i: (0, 0)),
            _mod_spec(lay, d, 0),
            _mod_spec(lay, d, 1),
            _resident((gd, 2 * gd), lambda i: (0, 0)),
        ],
        out_specs=[pl.BlockSpec((r, d), row), pl.BlockSpec((r, d), row)],
        out_shape=[jax.ShapeDtypeStruct((lay.N, d), BF16)] * 2,
        compiler_params=_params(("arbitrary",)),
        name="fourier_channel_dft",
    )(x, g, mods, mods, cs)


def _f2_kernel(ct_ref, st_ref, zc_ref, zs_ref, out_ref, *, t, rc):
    for c in range(t // rc):
        rs = slice(c * rc, (c + 1) * rc)
        y = jnp.dot(ct_ref[rs, :], zc_ref[...], preferred_element_type=F32)
        y = y - jnp.dot(st_ref[rs, :], zs_ref[...], preferred_element_type=F32)
        out_ref[rs, :] = y.astype(BF16)


def _f2(zc, zs, ct, st, batch, t, blk_off):
    d = zc.shape[1]
    tn = _col_chunk(d)
    rc = _col_chunk(t)
    zmap = lambda b, j: (blk_off + b, j)
    return pl.pallas_call(
        functools.partial(_f2_kernel, t=t, rc=rc),
        grid=(batch, d // tn),
        in_specs=[
            _resident((t, t), lambda b, j: (0, 0)),
            _resident((t, t), lambda b, j: (0, 0)),
            pl.BlockSpec((t, tn), zmap),
            pl.BlockSpec((t, tn), zmap),
        ],
        out_specs=pl.BlockSpec((t, tn), lambda b, j: (b, j)),
        out_shape=jax.ShapeDtypeStruct((batch * t, d), BF16),
        compiler_params=_params(("arbitrary", "arbitrary")),
        name="fourier_time_dft",
    )(ct, st, zc, zs)


def _pack_pairs(v):
    half = v.shape[1] // 2
    bits = pltpu.bitcast(v.astype(BF16).astype(F32), jnp.uint32)
    return (bits[:, :half] >> 16) | (bits[:, half:] & jnp.uint32(0xFFFF0000))


def _unpack_pairs(w):
    lo = pltpu.bitcast(w << 16, F32)
    hi = pltpu.bitcast(w & jnp.uint32(0xFFFF0000), F32)
    return lo, hi


def _router_kernel(x_ref, g_ref, sh_ref, sc_ref, rw_ref, rb_ref, tri_e_ref, tri_t_ref,
                   f_ref, ek_ref, rk_ref, wk_ref, cnt_ref, base_ref, *, n_exp):
    @pl.when(pl.program_id(0) == 0)
    def _():
        base_ref[...] = jnp.zeros_like(base_ref)

    f = _normmod(x_ref[...], g_ref[...], sh_ref[...], sc_ref[...])
    f_hi = f.astype(BF16)
    f_lo = (f - f_hi.astype(F32)).astype(BF16)
    f_ref[...] = _pack_pairs(f)
    rw = rw_ref[...]
    rw_hi = rw.astype(BF16)
    rw_lo = (rw - rw_hi.astype(F32)).astype(BF16)
    logits = _nt_dot(rw_hi, f_hi) + (_nt_dot(rw_hi, f_lo) + _nt_dot(rw_lo, f_hi))
    scores = jax.nn.sigmoid(logits)
    biased = scores + rb_ref[...]
    r = scores.shape[1]
    gsz = n_exp // N_GROUPS
    sub = lax.broadcasted_iota(jnp.int32, (gsz, r), 0)
    neg = jnp.float32(-jnp.inf)

    blocks = [biased[gi * gsz:(gi + 1) * gsz] for gi in range(N_GROUPS)]
    gscore = []
    for blk in blocks:
        m1 = jnp.max(blk, axis=0, keepdims=True)
        first = jnp.min(jnp.where(blk == m1, sub, gsz), axis=0, keepdims=True)
        m2 = jnp.max(jnp.where(sub == first, neg, blk), axis=0, keepdims=True)
        gscore.append(m1 + m2)
    masked = []
    for gi in range(N_GROUPS):
        rank = jnp.zeros((1, r), jnp.int32)
        for gj in range(N_GROUPS):
            if gj == gi:
                continue
            ahead = (gscore[gj] >= gscore[gi]) if gj < gi else (gscore[gj] > gscore[gi])
            rank = rank + ahead.astype(jnp.int32)
        masked.append(jnp.where(rank < TOPK_GROUPS, blocks[gi], neg))
    sel_f, sel_w = [], []
    for gi in range(N_GROUPS):
        blk = masked[gi]
        rank = jnp.zeros((gsz, r), jnp.int32)
        for gj in range(N_GROUPS):
            for j in range(gsz):
                other = masked[gj][j:j + 1]
                if gj < gi:
                    ahead = other >= blk
                elif gj > gi:
                    ahead = other > blk
                else:
                    ahead = (other > blk) | ((other == blk) & (sub > j))
                rank = rank + ahead.astype(jnp.int32)
        sel_f.append(jnp.where(rank < TOP_K, 1.0, 0.0))
        sel_w.append(jnp.where(rank < TOP_K, scores[gi * gsz:(gi + 1) * gsz], 0.0))
    tot = sel_w[0]
    for gi in range(1, N_GROUPS):
        tot = tot + sel_w[gi]
    denom = jnp.sum(tot, axis=0, keepdims=True)
    gates = jnp.concatenate(sel_w, axis=0) / denom * ROUTED_SCALE
    sel = jnp.concatenate(sel_f, axis=0)
    sel_b = sel.astype(BF16)
    slot = jnp.dot(tri_e_ref[...], sel_b, preferred_element_type=F32)
    base = base_ref[...]
    pos_in_expert = base + jnp.dot(sel_b, tri_t_ref[...], preferred_element_type=F32)
    base = base + jnp.sum(sel, axis=1, keepdims=True)
    base_ref[...] = base
    cnt_ref[...] = base
    e_idx = lax.broadcasted_iota(jnp.int32, sel.shape, 0).astype(F32)
    for k in range(TOP_K):
        hit = (sel > 0.0) & (slot == float(k))
        ek_ref[k:k + 1, :] = jnp.sum(jnp.where(hit, e_idx, 0.0), axis=0, keepdims=True).astype(jnp.int32)
        rk_ref[k:k + 1, :] = jnp.sum(jnp.where(hit, pos_in_expert, 0.0), axis=0, keepdims=True).astype(jnp.int32)
        wk_ref[k:k + 1, :] = jnp.sum(jnp.where(hit, gates, 0.0), axis=0, keepdims=True)


def _router(lay, x, g, mods, rw_t, rb):
    d = x.shape[1]
    n_exp = rw_t.shape[0]
    r = lay.R
    row = lambda i: (i, 0)
    tok = lambda i: (0, i)
    const = lambda i: (0, 0)
    tri_e = jnp.asarray(np.tril(np.ones((n_exp, n_exp), np.float32), -1)).astype(BF16)
    tri_t = jnp.asarray(np.triu(np.ones((r, r), np.float32), 1)).astype(BF16)
    return pl.pallas_call(
        functools.partial(_router_kernel, n_exp=n_exp),
        grid=(lay.ntiles,),
        in_specs=[
            pl.BlockSpec((r, d), row),
            _resident((1, d), const),
            _mod_spec(lay, d, 3),
            _mod_spec(lay, d, 4),
            _resident((n_exp, d), const),
            _resident((n_exp, 1), const),
            _resident((n_exp, n_exp), const),
            _resident((r, r), const),
        ],
        out_specs=[
            pl.BlockSpec((r, d // 2), row),
            pl.BlockSpec((TOP_K, r), tok),
            pl.BlockSpec((TOP_K, r), tok),
            pl.BlockSpec((TOP_K, r), tok),
            pl.BlockSpec((n_exp, 1), const),
        ],
        out_shape=[
            jax.ShapeDtypeStruct((lay.N, d // 2), jnp.uint32),
            jax.ShapeDtypeStruct((TOP_K, lay.N), jnp.int32),
            jax.ShapeDtypeStruct((TOP_K, lay.N), jnp.int32),
            jax.ShapeDtypeStruct((TOP_K, lay.N), F32),
            jax.ShapeDtypeStruct((n_exp, 1), F32),
        ],
        scratch_shapes=[pltpu.VMEM((n_exp, 1), F32)],
        compiler_params=_params(("arbitrary",)),
        name="moe_router",
    )(x, g, mods, mods, rw_t, rb, tri_e, tri_t)


def _dispatch_kernel(pad_start_ref, pad_len_ref, nused_ref, p_ref, f_ref, xs_ref, sem, zeros_ref,
                     *, n_exp, r, tmx):
    i = pl.program_id(0)

    def scatter_rows(t, carry):
        for k in range(TOP_K):
            p = p_ref[t * TOP_K + k]
            pltpu.make_async_copy(f_ref.at[pl.ds(t, 1)], xs_ref.at[pl.ds(p, 1)], sem.at[0]).start()
        return carry

    lax.fori_loop(0, r, scatter_rows, 0)
    pltpu.make_async_copy(xs_ref.at[pl.ds(0, r * TOP_K)], xs_ref.at[pl.ds(0, r * TOP_K)], sem.at[0]).wait()

    @pl.when(i == pl.num_programs(0) - 1)
    def _():
        zeros_ref[...] = jnp.zeros_like(zeros_ref)

        def pad_copies(e, act):
            start = pad_start_ref[e]
            length = pad_len_ref[e]
            head = jnp.minimum(length, (-start) & (V7X_SUBLANES - 1))
            base = start + head

            def row(j, c):
                act(pltpu.make_async_copy(zeros_ref.at[pl.ds(0, 1)], xs_ref.at[pl.ds(start + j, 1)], sem.at[1]))
                return c

            def group(j, c):
                off = pl.multiple_of(base + j * V7X_SUBLANES, V7X_SUBLANES)
                act(pltpu.make_async_copy(zeros_ref.at[pl.ds(0, V7X_SUBLANES)],
                                          xs_ref.at[pl.ds(off, V7X_SUBLANES)], sem.at[2]))
                return c

            lax.fori_loop(0, head, row, 0)
            lax.fori_loop(0, (length - head) // V7X_SUBLANES, group, 0)

        def tile_copy(m, act):
            dst = xs_ref.at[pl.ds(pl.multiple_of(m * tmx, tmx), tmx)]
            act(pltpu.make_async_copy(zeros_ref, dst, sem.at[3]))

        n_tiles = xs_ref.shape[0] // tmx
        for act in (lambda cp: cp.start(), lambda cp: cp.wait()):
            lax.fori_loop(0, n_exp, lambda e, c, act=act: (pad_copies(e, act), c)[1], 0)
            lax.fori_loop(nused_ref[0], n_tiles, lambda m, c, act=act: (tile_copy(m, act), c)[1], 0)


def _dispatch(lay, fpk, p_flat, pad_start, pad_len, n_used, n_rows, tmx):
    n, half = fpk.shape
    r = lay.R
    n_exp = pad_start.shape[0]
    grid_spec = pltpu.PrefetchScalarGridSpec(
        num_scalar_prefetch=3,
        grid=(lay.ntiles,),
        in_specs=[
            pl.BlockSpec((r * TOP_K,), lambda i, *_: (i,), memory_space=pltpu.SMEM),
            pl.BlockSpec((r, half), lambda i, *_: (i, 0)),
        ],
        out_specs=pl.BlockSpec(memory_space=pl.ANY),
        scratch_shapes=[pltpu.SemaphoreType.DMA((4,)), pltpu.VMEM((tmx, half), jnp.uint32)],
    )
    return pl.pallas_call(
        functools.partial(_dispatch_kernel, n_exp=n_exp, r=r, tmx=tmx),
        grid_spec=grid_spec,
        out_shape=jax.ShapeDtypeStruct((n_rows, half), jnp.uint32),
        compiler_params=_params(("arbitrary",)),
        name="moe_dispatch",
    )(pad_start, pad_len, n_used, p_flat, fpk)


def _swiglu_packed(xw, wg_ref, wu_ref, wd_ref):
    half = xw.shape[1]
    lo, hi = _unpack_pairs(xw)
    lo = lo.astype(BF16)
    hi = hi.astype(BF16)

    def proj(w_ref):
        top = jnp.dot(lo, w_ref[:half, :].astype(BF16), preferred_element_type=F32)
        return top + jnp.dot(hi, w_ref[half:, :].astype(BF16), preferred_element_type=F32)

    hid = (jax.nn.silu(proj(wg_ref)) * proj(wu_ref)).astype(BF16)
    return jnp.dot(hid, wd_ref[...].astype(BF16), preferred_element_type=F32)


def _grouped_kernel(te_ref, nused_ref, xs_ref, wg_ref, wu_ref, wd_ref, ys_ref):
    m = pl.program_id(0)

    @pl.when(m < nused_ref[0])
    def _():
        ys_ref[...] = _pack_pairs(_swiglu_packed(xs_ref[...], wg_ref, wu_ref, wd_ref))

    @pl.when(m >= nused_ref[0])
    def _():
        ys_ref[...] = jnp.zeros_like(ys_ref)


def _grouped(xs, wg, wu, wd, tile_expert, n_used, tmx):
    n_rows, half = xs.shape
    _, d, fdim = wg.shape
    grid_spec = pltpu.PrefetchScalarGridSpec(
        num_scalar_prefetch=2,
        grid=(n_rows // tmx,),
        in_specs=[
            pl.BlockSpec((tmx, half), lambda m, te, nu: (jnp.minimum(m, nu[0] - 1), 0)),
            pl.BlockSpec((None, d, fdim), lambda m, te, nu: (te[m], 0, 0)),
            pl.BlockSpec((None, d, fdim), lambda m, te, nu: (te[m], 0, 0)),
            pl.BlockSpec((None, fdim, d), lambda m, te, nu: (te[m], 0, 0)),
        ],
        out_specs=pl.BlockSpec((tmx, half), lambda m, te, nu: (m, 0)),
    )
    return pl.pallas_call(
        _grouped_kernel,
        grid_spec=grid_spec,
        out_shape=jax.ShapeDtypeStruct((n_rows, half), jnp.uint32),
        compiler_params=_params(("arbitrary",)),
        name="moe_experts",
    )(tile_expert, n_used, xs, wg, wu, wd)


def _combine_kernel(p_ref, w_ref, f_ref, x_ref, g2_ref, swg_ref, swu_ref, swd_ref, ys_ref, out_ref, buf_ref, sem,
                    *, r):
    def gather_rows(t, carry):
        for k in range(TOP_K):
            p = p_ref[t * TOP_K + k]
            pltpu.make_async_copy(ys_ref.at[pl.ds(p, 1)], buf_ref.at[k, pl.ds(t, 1)], sem.at[0]).start()
        return carry

    lax.fori_loop(0, r, gather_rows, 0)
    shared = _swiglu_packed(f_ref[...], swg_ref, swu_ref, swd_ref)
    pltpu.make_async_copy(buf_ref, buf_ref, sem.at[0]).wait()

    half = f_ref.shape[1]
    acc_lo = shared[:, :half]
    acc_hi = shared[:, half:]
    w = w_ref[...]
    for k in range(TOP_K):
        lo, hi = _unpack_pairs(buf_ref[k])
        wcol = w[:, k:k + 1]
        acc_lo = acc_lo + wcol * lo
        acc_hi = acc_hi + wcol * hi
    out_ref[:, :half] = x_ref[:, :half] + g2_ref[:, :half] * acc_lo
    out_ref[:, half:] = x_ref[:, half:] + g2_ref[:, half:] * acc_hi


def _combine(lay, ys, p_flat, w_tok, fpk, x, mods, swg, swu, swd):
    n, d = x.shape
    half = d // 2
    r = lay.R
    fdim = swd.shape[0]
    row = lambda i: (i, 0)
    const = lambda i: (0, 0)
    return pl.pallas_call(
        functools.partial(_combine_kernel, r=r),
        grid=(lay.ntiles,),
        in_specs=[
            pl.BlockSpec((r * TOP_K,), lambda i: (i,), memory_space=pltpu.SMEM),
            pl.BlockSpec((r, TOP_K), row),
            pl.BlockSpec((r, half), row),
            pl.BlockSpec((r, d), row),
            _mod_spec(lay, d, 5),
            _resident((d, fdim), const),
            _resident((d, fdim), const),
            _resident((fdim, d), const),
            pl.BlockSpec(memory_space=pl.ANY),
        ],
        out_specs=pl.BlockSpec((r, d), row),
        out_shape=jax.ShapeDtypeStruct((n, d), F32),
        scratch_shapes=[pltpu.VMEM((TOP_K, r, half), jnp.uint32), pltpu.SemaphoreType.DMA((1,))],
        compiler_params=_params(("arbitrary",)),
        name="moe_combine",
    )(p_flat, w_tok, fpk, x, mods, swg, swu, swd, ys)


def _moe(lay, x, g2n, mods, rw_t, rb, wg, wu, wd, swg, swu, swd):
    n_exp = rw_t.shape[0]
    tmx = 2 * lay.R
    fpk, ek, rk, wk, cnt = _router(lay, x, g2n, mods, rw_t, rb)
    cnt = cnt[:, 0].astype(jnp.int32)
    padded = (cnt + tmx - 1) // tmx * tmx
    offs = jnp.concatenate([jnp.zeros((1,), jnp.int32), jnp.cumsum(padded)])
    n_tiles = lay.N * TOP_K // tmx + n_exp
    e_ids = jnp.arange(n_exp, dtype=jnp.int32)[:, None, None]
    seg_start = jnp.sum(jnp.where(ek[None] > e_ids, padded[:, None, None], 0), axis=0)
    p_flat = (seg_start + rk).T.reshape(-1)
    w_tok = wk.T
    n_used = (offs[n_exp] // tmx).reshape(1)
    tile_start = jnp.minimum(jnp.arange(n_tiles, dtype=jnp.int32), n_used[0] - 1) * tmx
    tile_expert = jnp.sum((tile_start[:, None] >= offs[None, 1:]).astype(jnp.int32), axis=1)
    xs = _dispatch(lay, fpk, p_flat, offs[:-1] + cnt, padded - cnt, n_used, n_tiles * tmx, tmx)
    ys = _grouped(xs, wg, wu, wd, tile_expert, n_used, tmx)
    return _combine(lay, ys, p_flat, w_tok, fpk, x, mods, swg, swu, swd)


def _final_norm_kernel(x_ref, g_ref, out_ref):
    x = x_ref[...]
    ms = jnp.mean(x * x, axis=-1, keepdims=True)
    out_ref[...] = x * lax.rsqrt(ms + EPS) * g_ref[...]


def _final_norm(x, g, r):
    n, d = x.shape
    return pl.pallas_call(
        _final_norm_kernel,
        grid=(n // r,),
        in_specs=[pl.BlockSpec((r, d), lambda i: (i, 0)), _resident((1, d), lambda i: (0, 0))],
        out_specs=pl.BlockSpec((r, d), lambda i: (i, 0)),
        out_shape=jax.ShapeDtypeStruct((n, d), F32),
        compiler_params=_params(("arbitrary",)),
        name="final_norm",
    )(x, g)


def _grid_pos_embed(rows, dim):
    row = np.repeat(np.arange(rows, dtype=np.float32), GRID_W)
    col = np.tile(np.arange(GRID_W, dtype=np.float32), rows)
    quarter = dim // 4
    freqs = jnp.float32(POS_THETA) ** (-jnp.arange(quarter, dtype=F32) / quarter)
    ar = jnp.asarray(row)[:, None] * freqs
    ac = jnp.asarray(col)[:, None] * freqs
    return jnp.concatenate([jnp.sin(ar), jnp.cos(ar), jnp.sin(ac), jnp.cos(ac)], axis=-1).astype(F32)


def kernel(x, c, ctx, c_ctx, norm1_g, norm2_g, final_norm_g, w_mod, b_mod, rg_w_in, rg_conv_w, rg_conv_b, rg_w_a, rg_b_a, rg_w_i, rg_b_i, rg_lam, rg_w_out, fn_w_out, router_w, router_b, exp_w_gate, exp_w_up, exp_w_down, sh_w_gate, sh_w_up, sh_w_down):
    batch, t_lat, d = x.shape
    t_ctx = ctx.shape[1]
    depth = w_mod.shape[0]
    lay = _Layout(batch, t_ctx, t_lat, with_ctx=True)
    lay_lat = _Layout(batch, t_ctx, t_lat, with_ctx=False)

    n_cond = 1 + batch
    pad = (-n_cond) % V7X_SUBLANES
    cond = jnp.concatenate([c_ctx[None], c, jnp.zeros((pad, d), F32)], axis=0)
    mods_all = _ada(cond, w_mod, b_mod)

    pos = _grid_pos_embed(t_lat // GRID_W, d)
    xall = jnp.concatenate([ctx.reshape(batch * t_ctx, d), x.reshape(batch * t_lat, d)], axis=0)
    xall = _addpos(lay, xall, pos)

    gd = d // FN_GROUPS
    cg, sg = _dft_tables(gd)
    cs_tab = jnp.asarray(np.concatenate([cg, sg], axis=1)).astype(BF16)
    ct_lat, st_lat = (jnp.asarray(a).astype(BF16) for a in _dft_tables(t_lat))
    ct_ctx, st_ctx = (jnp.asarray(a).astype(BF16) for a in _dft_tables(t_ctx))

    for i in range(depth):
        last = i == depth - 1
        j = i // N_MIXERS
        mods = mods_all[i].reshape(mods_all.shape[1], 1, N_MOD * d)
        g1n = norm1_g[i][None]
        g2n = norm2_g[i][None]
        if i % N_MIXERS == 0:
            cur = lay
            gate, rec = _rg1(cur, xall, g1n, mods, rg_w_in[j].astype(BF16))
            hs = []
            for dr in range(2):
                hs.append(_rg2(cur, rec, rg_conv_w[j], rg_conv_b[j][None],
                               rg_w_a[j, dr].astype(BF16), rg_w_i[j, dr].astype(BF16),
                               rg_b_a[j, dr][None], rg_b_i[j, dr][None], rg_lam[j, dr][None],
                               rev=dr == 1))
            xall = _proj_res(cur, [hs[0], hs[1], gate], xall, mods, rg_w_out[j].astype(BF16))
        else:
            cur = lay_lat if last else lay
            x_off = lay.nct if last else 0
            zc, zs = _f1(cur, xall, g1n, mods, cs_tab, x_off=x_off)
            fre = [_f2(zc, zs, ct_lat, st_lat, batch, t_lat, cur.nct * cur.R // t_lat)]
            if not last:
                fre = [_f2(zc, zs, ct_ctx, st_ctx, batch, t_ctx, 0)] + fre
            xall = _proj_res(cur, fre, xall, mods, fn_w_out[j].astype(BF16), x_off=x_off)
        if last and cur is lay:
            xall = xall[lay.nct * lay.R:]
            cur = lay_lat
        xall = _moe(cur, xall, g2n, mods, router_w[i].T, router_b[i][:, None],
                    exp_w_gate[i], exp_w_up[i], exp_w_down[i], sh_w_gate[i], sh_w_up[i], sh_w_down[i])

    out = _final_norm(xall, final_norm_g[None], lay_lat.R)
    return out.reshape(batch, t_lat, d)
```

```python
import functools
import math

import numpy as np
import jax
import jax.numpy as jnp
from jax import lax
from jax.experimental import pallas as pl
from jax.experimental.pallas import tpu as pltpu

GRID_W = 64
N_MIXERS = 2
LRU_HEADS = 8
CONV_W = 4
CONV_LEFT = 2
RG_C = 8.0
FN_GROUPS = 8
TOP_K = 8
N_GROUPS = 8
TOPK_GROUPS = 4
ROUTED_SCALE = 2.5
N_MOD = 6
EPS = 1e-6
POS_THETA = 10000.0

V7X_VMEM_BYTES = 64 * 1024 * 1024
V7X_SUBLANES = 8
V7X_BF16_ROWS = 16
V7X_MXU_DIM = 256
MAX_ROW_TILE = 256
VMEM_LIMIT = 56 * 1024 * 1024

F32 = jnp.float32
BF16 = jnp.bfloat16


def _params(sem, vmem=VMEM_LIMIT):
    return pltpu.CompilerParams(dimension_semantics=sem, vmem_limit_bytes=vmem)


def _resident(shape, index_map):
    return pl.BlockSpec(shape, index_map, pipeline_mode=pl.Buffered(1))


def _col_chunk(n, cap=512):
    c = min(n, cap)
    while n % c:
        c //= 2
    return c


def _normmod(x, g, shift, scale):
    ms = jnp.mean(x * x, axis=-1, keepdims=True)
    y = x * lax.rsqrt(ms + EPS) * g
    return y * (1.0 + scale) + shift


def _nt_dot(a, b):
    return lax.dot_general(a, b, (((1,), (1,)), ((), ())), preferred_element_type=F32)


class _Layout:
    def __init__(self, batch, t_ctx, t_lat, with_ctx=True):
        g = math.gcd(t_ctx, t_lat)
        r = min(g, MAX_ROW_TILE)
        while g % r or r % V7X_BF16_ROWS:
            r -= 1
        assert r >= V7X_BF16_ROWS
        self.R = r
        self.B = batch
        self.Tc = t_ctx
        self.T = t_lat
        self.nC = t_ctx // r if with_ctx else 0
        self.nL = t_lat // r
        self.nct = batch * self.nC
        self.ntiles = self.nct + batch * self.nL
        self.N = self.ntiles * r
        assert (batch * t_ctx) % t_lat == 0 or not with_ctx

    def seg(self, i, tile=None):
        tile = self.R if tile is None else tile
        nct = self.nct * self.R // tile
        n_l = self.T // tile
        return jnp.where(i < nct, 0, 1 + (i - nct) // n_l)


def _mod_spec(lay, d, which, tile=None):
    return pl.BlockSpec((None, 1, d), lambda i, *_: (lay.seg(i, tile), 0, which))


def _ada_kernel(cond_ref, w_ref, b_ref, out_ref):
    s = jax.nn.silu(cond_ref[...]).astype(BF16)
    out_ref[...] = jnp.dot(s, w_ref[...].astype(BF16), preferred_element_type=F32) + b_ref[...]


def _ada(cond, w_mod, b_mod):
    depth, d, nm = w_mod.shape
    rows = cond.shape[0]
    tn = _col_chunk(nm, 1024)
    return pl.pallas_call(
        _ada_kernel,
        grid=(depth, nm // tn),
        in_specs=[
            pl.BlockSpec((rows, d), lambda l, j: (0, 0)),
            pl.BlockSpec((None, d, tn), lambda l, j: (l, 0, j)),
            pl.BlockSpec((None, 1, tn), lambda l, j: (l, 0, j)),
        ],
        out_specs=pl.BlockSpec((None, rows, tn), lambda l, j: (l, 0, j)),
        out_shape=jax.ShapeDtypeStruct((depth, rows, nm), F32),
        compiler_params=_params(("arbitrary", "arbitrary")),
        name="ada_mod",
    )(cond, w_mod, b_mod.reshape(depth, 1, nm))


def _addpos_kernel(ctx_ref, x_ref, pos_ref, out_ref, *, nct):
    i = pl.program_id(0)

    @pl.when(i < nct)
    def _():
        out_ref[...] = ctx_ref[...]

    @pl.when(i >= nct)
    def _():
        out_ref[...] = x_ref[...] + pos_ref[...]


def _addpos(lay, ctx2d, x2d, pos):
    d = x2d.shape[1]
    r = lay.R
    return pl.pallas_call(
        functools.partial(_addpos_kernel, nct=lay.nct),
        grid=(lay.ntiles,),
        in_specs=[
            pl.BlockSpec((r, d), lambda i: (jnp.minimum(i, lay.nct - 1), 0)),
            pl.BlockSpec((r, d), lambda i: (jnp.maximum(i - lay.nct, 0), 0)),
            pl.BlockSpec((r, d), lambda i: (jnp.maximum(i - lay.nct, 0) % lay.nL, 0)),
        ],
        out_specs=pl.BlockSpec((r, d), lambda i: (i, 0)),
        out_shape=jax.ShapeDtypeStruct((lay.N, d), F32),
        compiler_params=_params(("arbitrary",)),
        name="add_pos",
    )(ctx2d, x2d, pos)


def _rg1_kernel(x_ref, g_ref, sh_ref, sc_ref, w_ref, gate_ref, rec_ref, *, width, cn):
    h = _normmod(x_ref[...], g_ref[...], sh_ref[...], sc_ref[...]).astype(BF16)
    for c in range(2 * width // cn):
        z = jnp.dot(h, w_ref[:, c * cn:(c + 1) * cn], preferred_element_type=F32)
        if c * cn < width:
            gate_ref[:, c * cn:(c + 1) * cn] = jax.nn.gelu(z).astype(BF16)
        else:
            rec_ref[:, c * cn - width:(c + 1) * cn - width] = z.astype(BF16)


def _rg1(lay, x, g, mods, w_in):
    d = x.shape[1]
    width = w_in.shape[1] // 2
    r = lay.R
    cn = _col_chunk(width)
    row = lambda i: (i, 0)
    return pl.pallas_call(
        functools.partial(_rg1_kernel, width=width, cn=cn),
        grid=(lay.ntiles,),
        in_specs=[
            pl.BlockSpec((r, d), row),
            _resident((1, d), lambda i: (0, 0)),
            _mod_spec(lay, d, 0),
            _mod_spec(lay, d, 1),
            _resident((d, 2 * width), lambda i: (0, 0)),
        ],
        out_specs=[pl.BlockSpec((r, width), row), pl.BlockSpec((r, width), row)],
        out_shape=[jax.ShapeDtypeStruct((lay.N, width), BF16)] * 2,
        compiler_params=_params(("arbitrary",)),
        name="rg_in_proj",
    )(x, g, mods, mods, w_in)


def _log_sigmoid(x):
    return jnp.minimum(x, 0.0) - jnp.log1p(jnp.exp(-jnp.abs(x)))


def _neg_expm1(t, exp_t):
    series = t * (1.0 + t * (0.5 + t * (1.0 / 6.0 + t * (1.0 / 24.0 + t * (1.0 / 120.0)))))
    return -jnp.where(t > -0.1, series, exp_t - 1.0)


def _rg2_kernel(cur_ref, prev_ref, next_ref, cw_ref, cb_ref, wa_ref, wi_ref, ba_ref, bi_ref,
                lam_ref, out_ref, carry_ref, *, rev, n_c, n_l, r, hd, heads):
    s = pl.program_id(1)

    @pl.when(s == 0)
    def _():
        carry_ref[...] = jnp.zeros_like(carry_ref)

    in_ctx = s < n_c
    pos = jnp.where(in_ctx, s, s - n_c)
    n_seq = jnp.where(in_ctx, n_c, n_l)
    chunk = (n_seq - 1 - pos) if rev else pos
    keep_prev = jnp.where(chunk == 0, 0.0, 1.0).astype(F32)
    keep_next = jnp.where(chunk == n_seq - 1, 0.0, 1.0).astype(F32)

    row = lax.broadcasted_iota(jnp.int32, (r, hd), 0)
    sub = row % V7X_SUBLANES
    hrows = V7X_BF16_ROWS
    groups = r // V7X_SUBLANES

    for h in range(heads):
        sl = slice(h * hd, (h + 1) * hd)
        cur = cur_ref[:, sl].astype(F32)
        pv = prev_ref[:, sl].astype(F32) * keep_prev
        nx = next_ref[:, sl].astype(F32) * keep_next
        p_m2 = pv[hrows - 2:hrows - 1, :]
        p_m1 = pv[hrows - 1:hrows, :]
        n_p1 = nx[0:1, :]
        x_m1 = jnp.where(row == 0, p_m1, pltpu.roll(cur, 1, 0))
        x_m2 = jnp.where(row == 0, p_m2, jnp.where(row == 1, p_m1, pltpu.roll(cur, 2, 0)))
        x_p1 = jnp.where(row == r - 1, n_p1, pltpu.roll(cur, r - 1, 0))
        u = (cw_ref[0:1, sl] * x_m2 + cw_ref[1:2, sl] * x_m1 + cw_ref[2:3, sl] * cur
             + cw_ref[3:4, sl] * x_p1 + cb_ref[:, sl])
        ub = u.astype(BF16)
        ra = jnp.dot(ub, wa_ref[h], preferred_element_type=F32) + ba_ref[:, sl]
        ri = jnp.dot(ub, wi_ref[h], preferred_element_type=F32) + bi_ref[:, sl]
        log_a = RG_C * jax.nn.sigmoid(ra) * _log_sigmoid(lam_ref[:, sl])
        a = jnp.exp(log_a)
        b = jnp.sqrt(_neg_expm1(2.0 * log_a, a * a)) * (jax.nn.sigmoid(ri) * u)

        for sft in (1, 2, 4):
            if rev:
                a_s = pltpu.roll(a, r - sft, 0)
                b_s = pltpu.roll(b, r - sft, 0)
                m = sub < V7X_SUBLANES - sft
            else:
                a_s = pltpu.roll(a, sft, 0)
                b_s = pltpu.roll(b, sft, 0)
                m = sub >= sft
            b = jnp.where(m, a * b_s + b, b)
            a = jnp.where(m, a * a_s, a)

        carry = carry_ref[:, sl]
        outs = [None] * groups
        order = range(groups - 1, -1, -1) if rev else range(groups)
        for gi in order:
            lo = gi * V7X_SUBLANES
            hg = b[lo:lo + V7X_SUBLANES] + a[lo:lo + V7X_SUBLANES] * carry
            carry = hg[0:1] if rev else hg[V7X_SUBLANES - 1:V7X_SUBLANES]
            outs[gi] = hg
        carry_ref[:, sl] = carry
        out_ref[:, sl] = jnp.concatenate(outs, axis=0).astype(BF16)


def _rg2(lay, rec, conv_w, conv_b, w_a, w_i, b_a, b_i, lam, rev):
    n, width = rec.shape
    r, n_c, n_l = lay.R, lay.nC, lay.nL
    heads = w_a.shape[0]
    hd = width // heads
    hr = V7X_BF16_ROWS
    rb = r // hr
    last_hb = n // hr - 1

    def tile(b, s):
        if rev:
            c_t = b * n_c + (n_c - 1 - s)
            l_t = lay.nct + b * n_l + (n_l - 1 - (s - n_c))
        else:
            c_t = b * n_c + s
            l_t = lay.nct + b * n_l + (s - n_c)
        return jnp.where(s < n_c, c_t, l_t)

    const2 = lambda b, s: (0, 0)
    const3 = lambda b, s: (0, 0, 0)
    return pl.pallas_call(
        functools.partial(_rg2_kernel, rev=rev, n_c=n_c, n_l=n_l, r=r, hd=hd, heads=heads),
        grid=(lay.B, n_c + n_l),
        in_specs=[
            pl.BlockSpec((r, width), lambda b, s: (tile(b, s), 0)),
            pl.BlockSpec((hr, width), lambda b, s: (jnp.maximum(tile(b, s) * rb - 1, 0), 0)),
            pl.BlockSpec((hr, width), lambda b, s: (jnp.minimum((tile(b, s) + 1) * rb, last_hb), 0)),
            _resident((CONV_W, width), const2),
            _resident((1, width), const2),
            _resident((heads, hd, hd), const3),
            _resident((heads, hd, hd), const3),
            _resident((1, width), const2),
            _resident((1, width), const2),
            _resident((1, width), const2),
        ],
        out_specs=pl.BlockSpec((r, width), lambda b, s: (tile(b, s), 0)),
        out_shape=jax.ShapeDtypeStruct((n, width), BF16),
        scratch_shapes=[pltpu.VMEM((1, width), F32)],
        compiler_params=_params(("arbitrary", "arbitrary")),
        name="rg_scan_bwd" if rev else "rg_scan_fwd",
    )(rec, rec, rec, conv_w, conv_b, w_a, w_i, b_a, b_i, lam)


def _proj_res_kernel(*refs, mode, nct, cn):
    if mode == "gated":
        hf_ref, hb_ref, gate_ref, x_ref, g1_ref, w_ref, out_ref = refs
        a = ((hf_ref[...].astype(F32) + hb_ref[...].astype(F32)) * gate_ref[...].astype(F32)).astype(BF16)
    elif mode == "ctx_lat":
        ac_ref, al_ref, x_ref, g1_ref, w_ref, out_ref = refs
        a = jnp.where(pl.program_id(0) < nct, ac_ref[...], al_ref[...])
    else:
        a_ref, x_ref, g1_ref, w_ref, out_ref = refs
        a = a_ref[...]
    d = out_ref.shape[1]
    for c in range(d // cn):
        cs = slice(c * cn, (c + 1) * cn)
        y = jnp.dot(a, w_ref[:, cs], preferred_element_type=F32)
        out_ref[:, cs] = x_ref[:, cs] + g1_ref[:, cs] * y


def _proj_res(lay, a_list, x, mods, w_out, x_off=0):
    k, d = w_out.shape
    r = lay.R
    cn = _col_chunk(d)
    row = lambda i: (i, 0)
    mode = {1: "plain", 2: "ctx_lat", 3: "gated"}[len(a_list)]
    if mode == "ctx_lat":
        a_maps = [lambda i: (jnp.minimum(i, lay.nct - 1), 0), lambda i: (jnp.maximum(i - lay.nct, 0), 0)]
    else:
        a_maps = [row] * len(a_list)
    return pl.pallas_call(
        functools.partial(_proj_res_kernel, mode=mode, nct=lay.nct, cn=cn),
        grid=(lay.ntiles,),
        in_specs=[pl.BlockSpec((r, k), m) for m in a_maps] + [
            pl.BlockSpec((r, d), lambda i: (i + x_off, 0)),
            _mod_spec(lay, d, 2),
            _resident((k, d), lambda i: (0, 0)),
        ],
        out_specs=pl.BlockSpec((r, d), row),
        out_shape=jax.ShapeDtypeStruct((lay.N, d), F32),
        compiler_params=_params(("arbitrary",)),
        name="mixer_out_proj",
    )(*a_list, x, mods, w_out)


def _dft_tables(n):
    k = np.arange(n, dtype=np.int64)
    ang = 2.0 * np.pi * ((k[:, None] * k[None, :]) % n).astype(np.float64) / n
    s = 1.0 / np.sqrt(n)
    return (np.cos(ang) * s).astype(np.float32), (np.sin(ang) * s).astype(np.float32)


def _f1_kernel(x_ref, g_ref, sh_ref, sc_ref, cs_ref, zc_ref, zs_ref, *, groups, gd):
    h = _normmod(x_ref[...], g_ref[...], sh_ref[...], sc_ref[...]).astype(BF16)
    for gi in range(groups):
        sl = slice(gi * gd, (gi + 1) * gd)
        z = jnp.dot(h[:, sl], cs_ref[...], preferred_element_type=F32)
        zc_ref[:, sl] = z[:, :gd].astype(BF16)
        zs_ref[:, sl] = z[:, gd:].astype(BF16)


def _f1(lay, x, g, mods, cs, x_off=0):
    d = x.shape[1]
    r = lay.R
    gd = d // FN_GROUPS
    row = lambda i: (i, 0)
    return pl.pallas_call(
        functools.partial(_f1_kernel, groups=FN_GROUPS, gd=gd),
        grid=(lay.ntiles,),
        in_specs=[
            pl.BlockSpec((r, d), lambda i: (i + x_off, 0)),
            _resident((1, d), lambda i: (0, 0)),
            _mod_spec(lay, d, 0),
            _mod_spec(lay, d, 1),
            _resident((gd, 2 * gd), lambda i: (0, 0)),
        ],
        out_specs=[pl.BlockSpec((r, d), row), pl.BlockSpec((r, d), row)],
        out_shape=[jax.ShapeDtypeStruct((lay.N, d), BF16)] * 2,
        compiler_params=_params(("arbitrary",)),
        name="fourier_channel_dft",
    )(x, g, mods, mods, cs)


def _f2_kernel(ct_ref, st_ref, zc_ref, zs_ref, out_ref, *, t, rc):
    for c in range(t // rc):
        rs = slice(c * rc, (c + 1) * rc)
        y = jnp.dot(ct_ref[rs, :], zc_ref[...], preferred_element_type=F32)
        y = y - jnp.dot(st_ref[rs, :], zs_ref[...], preferred_element_type=F32)
        out_ref[rs, :] = y.astype(BF16)


def _f2(zc, zs, ct, st, batch, t, blk_off):
    d = zc.shape[1]
    tn = _col_chunk(d)
    rc = _col_chunk(t)
    zmap = lambda b, j: (blk_off + b, j)
    return pl.pallas_call(
        functools.partial(_f2_kernel, t=t, rc=rc),
        grid=(batch, d // tn),
        in_specs=[
            _resident((t, t), lambda b, j: (0, 0)),
            _resident((t, t), lambda b, j: (0, 0)),
            pl.BlockSpec((t, tn), zmap),
            pl.BlockSpec((t, tn), zmap),
        ],
        out_specs=pl.BlockSpec((t, tn), lambda b, j: (b, j)),
        out_shape=jax.ShapeDtypeStruct((batch * t, d), BF16),
        compiler_params=_params(("arbitrary", "arbitrary")),
        name="fourier_time_dft",
    )(ct, st, zc, zs)


def _pack_pairs(v):
    half = v.shape[1] // 2
    bits = pltpu.bitcast(v.astype(BF16).astype(F32), jnp.uint32)
    return (bits[:, :half] >> 16) | (bits[:, half:] & jnp.uint32(0xFFFF0000))


def _unpack_pairs(w):
    lo = pltpu.bitcast(w << 16, F32)
    hi = pltpu.bitcast(w & jnp.uint32(0xFFFF0000), F32)
    return lo, hi


def _router_kernel(x_ref, g_ref, sh_ref, sc_ref, rw_ref, rb_ref, tri_e_ref, tri_t_ref,
                   f_ref, ek_ref, rk_ref, wk_ref, cnt_ref, base_ref, *, n_exp):
    @pl.when(pl.program_id(0) == 0)
    def _():
        base_ref[...] = jnp.zeros_like(base_ref)

    f = _normmod(x_ref[...], g_ref[...], sh_ref[...], sc_ref[...])
    f_hi = f.astype(BF16)
    f_lo = (f - f_hi.astype(F32)).astype(BF16)
    f_ref[...] = _pack_pairs(f)
    rw = rw_ref[...]
    rw_hi = rw.astype(BF16)
    rw_lo = (rw - rw_hi.astype(F32)).astype(BF16)
    logits = _nt_dot(rw_hi, f_hi) + (_nt_dot(rw_hi, f_lo) + _nt_dot(rw_lo, f_hi))
    scores = jax.nn.sigmoid(logits)
    biased = scores + rb_ref[...]
    r = scores.shape[1]
    gsz = n_exp // N_GROUPS
    sub = lax.broadcasted_iota(jnp.int32, (gsz, r), 0)
    neg = jnp.float32(-jnp.inf)

    blocks = [biased[gi * gsz:(gi + 1) * gsz] for gi in range(N_GROUPS)]
    gscore = []
    for blk in blocks:
        m1 = jnp.max(blk, axis=0, keepdims=True)
        first = jnp.min(jnp.where(blk == m1, sub, gsz), axis=0, keepdims=True)
        m2 = jnp.max(jnp.where(sub == first, neg, blk), axis=0, keepdims=True)
        gscore.append(m1 + m2)
    masked = []
    for gi in range(N_GROUPS):
        rank = jnp.zeros((1, r), jnp.int32)
        for gj in range(N_GROUPS):
            if gj == gi:
                continue
            ahead = (gscore[gj] >= gscore[gi]) if gj < gi else (gscore[gj] > gscore[gi])
            rank = rank + ahead.astype(jnp.int32)
        masked.append(jnp.where(rank < TOPK_GROUPS, blocks[gi], neg))
    sel_f, sel_w = [], []
    for gi in range(N_GROUPS):
        blk = masked[gi]
        rank = jnp.zeros((gsz, r), jnp.int32)
        for gj in range(N_GROUPS):
            for j in range(gsz):
                other = masked[gj][j:j + 1]
                if gj < gi:
                    ahead = other >= blk
                elif gj > gi:
                    ahead = other > blk
                else:
                    ahead = (other > blk) | ((other == blk) & (sub > j))
                rank = rank + ahead.astype(jnp.int32)
        sel_f.append(jnp.where(rank < TOP_K, 1.0, 0.0))
        sel_w.append(jnp.where(rank < TOP_K, scores[gi * gsz:(gi + 1) * gsz], 0.0))
    tot = sel_w[0]
    for gi in range(1, N_GROUPS):
        tot = tot + sel_w[gi]
    denom = jnp.sum(tot, axis=0, keepdims=True)
    gates = jnp.concatenate(sel_w, axis=0) / denom * ROUTED_SCALE
    sel = jnp.concatenate(sel_f, axis=0)
    sel_b = sel.astype(BF16)
    slot = jnp.dot(tri_e_ref[...], sel_b, preferred_element_type=F32)
    base = base_ref[...]
    pos_in_expert = base + jnp.dot(sel_b, tri_t_ref[...], preferred_element_type=F32)
    base = base + jnp.sum(sel, axis=1, keepdims=True)
    base_ref[...] = base
    cnt_ref[...] = base
    e_idx = lax.broadcasted_iota(jnp.int32, sel.shape, 0).astype(F32)
    for k in range(TOP_K):
        hit = (sel > 0.0) & (slot == float(k))
        ek_ref[k:k + 1, :] = jnp.sum(jnp.where(hit, e_idx, 0.0), axis=0, keepdims=True).astype(jnp.int32)
        rk_ref[k:k + 1, :] = jnp.sum(jnp.where(hit, pos_in_expert, 0.0), axis=0, keepdims=True).astype(jnp.int32)
        wk_ref[k:k + 1, :] = jnp.sum(jnp.where(hit, gates, 0.0), axis=0, keepdims=True)


def _router(lay, x, g, mods, rw_t, rb):
    d = x.shape[1]
    n_exp = rw_t.shape[0]
    r = lay.R
    row = lambda i: (i, 0)
    tok = lambda i: (0, i)
    const = lambda i: (0, 0)
    tri_e = jnp.asarray(np.tril(np.ones((n_exp, n_exp), np.float32), -1)).astype(BF16)
    tri_t = jnp.asarray(np.triu(np.ones((r, r), np.float32), 1)).astype(BF16)
    return pl.pallas_call(
        functools.partial(_router_kernel, n_exp=n_exp),
        grid=(lay.ntiles,),
        in_specs=[
            pl.BlockSpec((r, d), row),
            _resident((1, d), const),
            _mod_spec(lay, d, 3),
            _mod_spec(lay, d, 4),
            _resident((n_exp, d), const),
            _resident((n_exp, 1), const),
            _resident((n_exp, n_exp), const),
            _resident((r, r), const),
        ],
        out_specs=[
            pl.BlockSpec((r, d // 2), row),
            pl.BlockSpec((TOP_K, r), tok),
            pl.BlockSpec((TOP_K, r), tok),
            pl.BlockSpec((TOP_K, r), tok),
            pl.BlockSpec((n_exp, 1), const),
        ],
        out_shape=[
            jax.ShapeDtypeStruct((lay.N, d // 2), jnp.uint32),
            jax.ShapeDtypeStruct((TOP_K, lay.N), jnp.int32),
            jax.ShapeDtypeStruct((TOP_K, lay.N), jnp.int32),
            jax.ShapeDtypeStruct((TOP_K, lay.N), F32),
            jax.ShapeDtypeStruct((n_exp, 1), F32),
        ],
        scratch_shapes=[pltpu.VMEM((n_exp, 1), F32)],
        compiler_params=_params(("arbitrary",)),
        name="moe_router",
    )(x, g, mods, mods, rw_t, rb, tri_e, tri_t)


def _dispatch_kernel(pad_start_ref, pad_len_ref, nused_ref, p_ref, f_ref, xs_ref, sem, zeros_ref,
                     *, n_exp, r, tmx):
    i = pl.program_id(0)

    def scatter_rows(t, carry):
        for k in range(TOP_K):
            p = p_ref[t * TOP_K + k]
            pltpu.make_async_copy(f_ref.at[pl.ds(t, 1)], xs_ref.at[pl.ds(p, 1)], sem.at[0]).start()
        return carry

    lax.fori_loop(0, r, scatter_rows, 0)
    pltpu.make_async_copy(xs_ref.at[pl.ds(0, r * TOP_K)], xs_ref.at[pl.ds(0, r * TOP_K)], sem.at[0]).wait()

    @pl.when(i == pl.num_programs(0) - 1)
    def _():
        zeros_ref[...] = jnp.zeros_like(zeros_ref)

        def pad_copies(e, act):
            start = pad_start_ref[e]
            length = pad_len_ref[e]
            head = jnp.minimum(length, (-start) & (V7X_SUBLANES - 1))
            base = start + head

            def row(j, c):
                act(pltpu.make_async_copy(zeros_ref.at[pl.ds(0, 1)], xs_ref.at[pl.ds(start + j, 1)], sem.at[1]))
                return c

            def group(j, c):
                off = pl.multiple_of(base + j * V7X_SUBLANES, V7X_SUBLANES)
                act(pltpu.make_async_copy(zeros_ref.at[pl.ds(0, V7X_SUBLANES)],
                                          xs_ref.at[pl.ds(off, V7X_SUBLANES)], sem.at[2]))
                return c

            lax.fori_loop(0, head, row, 0)
            lax.fori_loop(0, (length - head) // V7X_SUBLANES, group, 0)

        def tile_copy(m, act):
            dst = xs_ref.at[pl.ds(pl.multiple_of(m * tmx, tmx), tmx)]
            act(pltpu.make_async_copy(zeros_ref, dst, sem.at[3]))

        n_tiles = xs_ref.shape[0] // tmx
        for act in (lambda cp: cp.start(), lambda cp: cp.wait()):
            lax.fori_loop(0, n_exp, lambda e, c, act=act: (pad_copies(e, act), c)[1], 0)
            lax.fori_loop(nused_ref[0], n_tiles, lambda m, c, act=act: (tile_copy(m, act), c)[1], 0)


def _dispatch(lay, fpk, p_flat, pad_start, pad_len, n_used, n_rows, tmx):
    n, half = fpk.shape
    r = lay.R
    n_exp = pad_start.shape[0]
    grid_spec = pltpu.PrefetchScalarGridSpec(
        num_scalar_prefetch=3,
        grid=(lay.ntiles,),
        in_specs=[
            pl.BlockSpec((r * TOP_K,), lambda i, *_: (i,), memory_space=pltpu.SMEM),
            pl.BlockSpec((r, half), lambda i, *_: (i, 0)),
        ],
        out_specs=pl.BlockSpec(memory_space=pl.ANY),
        scratch_shapes=[pltpu.SemaphoreType.DMA((4,)), pltpu.VMEM((tmx, half), jnp.uint32)],
    )
    return pl.pallas_call(
        functools.partial(_dispatch_kernel, n_exp=n_exp, r=r, tmx=tmx),
        grid_spec=grid_spec,
        out_shape=jax.ShapeDtypeStruct((n_rows, half), jnp.uint32),
        compiler_params=_params(("arbitrary",)),
        name="moe_dispatch",
    )(pad_start, pad_len, n_used, p_flat, fpk)


def _swiglu_packed(xw, wg_ref, wu_ref, wd_ref):
    half = xw.shape[1]
    lo, hi = _unpack_pairs(xw)
    lo = lo.astype(BF16)
    hi = hi.astype(BF16)

    def proj(w_ref):
        top = jnp.dot(lo, w_ref[:half, :], preferred_element_type=F32)
        return top + jnp.dot(hi, w_ref[half:, :], preferred_element_type=F32)

    hid = (jax.nn.silu(proj(wg_ref)) * proj(wu_ref)).astype(BF16)
    return jnp.dot(hid, wd_ref[...], preferred_element_type=F32)


def _grouped_kernel(te_ref, nused_ref, xs_ref, wg_ref, wu_ref, wd_ref, ys_ref, wg_b, wu_b, wd_b):
    m = pl.program_id(0)

    @pl.when((m == 0) | (te_ref[m] != te_ref[jnp.maximum(m - 1, 0)]))
    def _():
        wg_b[...] = wg_ref[...].astype(BF16)
        wu_b[...] = wu_ref[...].astype(BF16)
        wd_b[...] = wd_ref[...].astype(BF16)

    @pl.when(m < nused_ref[0])
    def _():
        ys_ref[...] = _pack_pairs(_swiglu_packed(xs_ref[...], wg_b, wu_b, wd_b))

    @pl.when(m >= nused_ref[0])
    def _():
        ys_ref[...] = jnp.zeros_like(ys_ref)


def _grouped(xs, wg, wu, wd, tile_expert, n_used, tmx):
    n_rows, half = xs.shape
    _, d, fdim = wg.shape
    grid_spec = pltpu.PrefetchScalarGridSpec(
        num_scalar_prefetch=2,
        grid=(n_rows // tmx,),
        in_specs=[
            pl.BlockSpec((tmx, half), lambda m, te, nu: (jnp.minimum(m, nu[0] - 1), 0)),
            pl.BlockSpec((None, d, fdim), lambda m, te, nu: (te[m], 0, 0)),
            pl.BlockSpec((None, d, fdim), lambda m, te, nu: (te[m], 0, 0)),
            pl.BlockSpec((None, fdim, d), lambda m, te, nu: (te[m], 0, 0)),
        ],
        out_specs=pl.BlockSpec((tmx, half), lambda m, te, nu: (m, 0)),
        scratch_shapes=[pltpu.VMEM((d, fdim), BF16), pltpu.VMEM((d, fdim), BF16), pltpu.VMEM((fdim, d), BF16)],
    )
    return pl.pallas_call(
        _grouped_kernel,
        grid_spec=grid_spec,
        out_shape=jax.ShapeDtypeStruct((n_rows, half), jnp.uint32),
        compiler_params=_params(("arbitrary",)),
        name="moe_experts",
    )(tile_expert, n_used, xs, wg, wu, wd)


def _combine_kernel(p0_ref, pn_ref, w_ref, f_ref, x_ref, g2_ref, swg_ref, swu_ref, swd_ref, ys_ref, out_ref,
                    buf_a, buf_b, sh_ref, sem, *, r):
    i = pl.program_id(0)
    last = pl.num_programs(0) - 1
    half = f_ref.shape[1]
    rows8 = V7X_SUBLANES

    def issue(p_ref, dst, dsem, t0, count):
        for j in range(count):
            t = t0 + j
            for k in range(TOP_K):
                p = p_ref[t * TOP_K + k]
                pltpu.make_async_copy(ys_ref.at[pl.ds(p, 1)], dst.at[k, pl.ds(t, 1)], dsem).start()

    @pl.when(i == 0)
    def _():
        lax.fori_loop(0, r, lambda t, c: (issue(p0_ref, buf_a, sem.at[0], t, 1), c)[1], 0)

    def run(cur, cur_sem, nxt, nxt_sem):
        pltpu.make_async_copy(cur, cur, cur_sem).wait()
        sh_ref[...] = _swiglu_packed(f_ref[...], swg_ref, swu_ref, swd_ref)

        def finish_rows(g):
            rows = pl.ds(pl.multiple_of(g * rows8, rows8), rows8)
            acc_lo = sh_ref[rows, :half]
            acc_hi = sh_ref[rows, half:]
            w = w_ref[rows, :]
            for k in range(TOP_K):
                lo, hi = _unpack_pairs(cur[k, rows, :])
                wcol = w[:, k:k + 1]
                acc_lo = acc_lo + wcol * lo
                acc_hi = acc_hi + wcol * hi
            out_ref[rows, :half] = x_ref[rows, :half] + g2_ref[:, :half] * acc_lo
            out_ref[rows, half:] = x_ref[rows, half:] + g2_ref[:, half:] * acc_hi

        @pl.when(i < last)
        def _():
            def body(g, c):
                issue(pn_ref, nxt, nxt_sem, pl.multiple_of(g * rows8, rows8), rows8)
                finish_rows(g)
                return c

            lax.fori_loop(0, r // rows8, body, 0)

        @pl.when(i == last)
        def _():
            lax.fori_loop(0, r // rows8, lambda g, c: (finish_rows(g), c)[1], 0)

    @pl.when(i % 2 == 0)
    def _():
        run(buf_a, sem.at[0], buf_b, sem.at[1])

    @pl.when(i % 2 == 1)
    def _():
        run(buf_b, sem.at[1], buf_a, sem.at[0])


def _combine(lay, ys, p_flat, w_tok, fpk, x, mods, swg, swu, swd):
    n, d = x.shape
    half = d // 2
    r = lay.R
    fdim = swd.shape[0]
    row = lambda i: (i, 0)
    const = lambda i: (0, 0)
    gather_buf = pltpu.VMEM((TOP_K, r, half), jnp.uint32)
    return pl.pallas_call(
        functools.partial(_combine_kernel, r=r),
        grid=(lay.ntiles,),
        in_specs=[
            pl.BlockSpec((r * TOP_K,), lambda i: (i,), memory_space=pltpu.SMEM),
            pl.BlockSpec((r * TOP_K,), lambda i: (jnp.minimum(i + 1, lay.ntiles - 1),), memory_space=pltpu.SMEM),
            pl.BlockSpec((r, TOP_K), row),
            pl.BlockSpec((r, half), row),
            pl.BlockSpec((r, d), row),
            _mod_spec(lay, d, 5),
            _resident((d, fdim), const),
            _resident((d, fdim), const),
            _resident((fdim, d), const),
            pl.BlockSpec(memory_space=pl.ANY),
        ],
        out_specs=pl.BlockSpec((r, d), row),
        out_shape=jax.ShapeDtypeStruct((n, d), F32),
        scratch_shapes=[gather_buf, gather_buf, pltpu.VMEM((r, d), F32), pltpu.SemaphoreType.DMA((2,))],
        compiler_params=_params(("arbitrary",)),
        name="moe_combine",
    )(p_flat, p_flat, w_tok, fpk, x, mods, swg, swu, swd, ys)


def _moe(lay, x, g2n, mods, rw_t, rb, wg, wu, wd, swg, swu, swd):
    n_exp = rw_t.shape[0]
    tmx = 2 * lay.R
    fpk, ek, rk, wk, cnt = _router(lay, x, g2n, mods, rw_t, rb)
    cnt = cnt[:, 0].astype(jnp.int32)
    padded = (cnt + tmx - 1) // tmx * tmx
    offs = jnp.concatenate([jnp.zeros((1,), jnp.int32), jnp.cumsum(padded)])
    n_tiles = lay.N * TOP_K // tmx + n_exp
    e_ids = jnp.arange(n_exp, dtype=jnp.int32)[:, None, None]
    seg_start = jnp.sum(jnp.where(ek[None] > e_ids, padded[:, None, None], 0), axis=0)
    p_flat = (seg_start + rk).T.reshape(-1)
    w_tok = wk.T
    n_used = (offs[n_exp] // tmx).reshape(1)
    tile_start = jnp.minimum(jnp.arange(n_tiles, dtype=jnp.int32), n_used[0] - 1) * tmx
    tile_expert = jnp.sum((tile_start[:, None] >= offs[None, 1:]).astype(jnp.int32), axis=1)
    xs = _dispatch(lay, fpk, p_flat, offs[:-1] + cnt, padded - cnt, n_used, n_tiles * tmx, tmx)
    ys = _grouped(xs, wg, wu, wd, tile_expert, n_used, tmx)
    return _combine(lay, ys, p_flat, w_tok, fpk, x, mods, swg.astype(BF16), swu.astype(BF16), swd.astype(BF16))


def _final_norm_kernel(x_ref, g_ref, out_ref):
    x = x_ref[...]
    ms = jnp.mean(x * x, axis=-1, keepdims=True)
    out_ref[...] = x * lax.rsqrt(ms + EPS) * g_ref[...]


def _final_norm(x, g, r):
    n, d = x.shape
    return pl.pallas_call(
        _final_norm_kernel,
        grid=(n // r,),
        in_specs=[pl.BlockSpec((r, d), lambda i: (i, 0)), _resident((1, d), lambda i: (0, 0))],
        out_specs=pl.BlockSpec((r, d), lambda i: (i, 0)),
        out_shape=jax.ShapeDtypeStruct((n, d), F32),
        compiler_params=_params(("arbitrary",)),
        name="final_norm",
    )(x, g)


def _grid_pos_embed(rows, dim):
    row = np.repeat(np.arange(rows, dtype=np.float32), GRID_W)
    col = np.tile(np.arange(GRID_W, dtype=np.float32), rows)
    quarter = dim // 4
    freqs = jnp.float32(POS_THETA) ** (-jnp.arange(quarter, dtype=F32) / quarter)
    ar = jnp.asarray(row)[:, None] * freqs
    ac = jnp.asarray(col)[:, None] * freqs
    return jnp.concatenate([jnp.sin(ar), jnp.cos(ar), jnp.sin(ac), jnp.cos(ac)], axis=-1).astype(F32)


def kernel(x, c, ctx, c_ctx, norm1_g, norm2_g, final_norm_g, w_mod, b_mod, rg_w_in, rg_conv_w, rg_conv_b, rg_w_a, rg_b_a, rg_w_i, rg_b_i, rg_lam, rg_w_out, fn_w_out, router_w, router_b, exp_w_gate, exp_w_up, exp_w_down, sh_w_gate, sh_w_up, sh_w_down):
    batch, t_lat, d = x.shape
    t_ctx = ctx.shape[1]
    depth = w_mod.shape[0]
    lay = _Layout(batch, t_ctx, t_lat, with_ctx=True)
    lay_lat = _Layout(batch, t_ctx, t_lat, with_ctx=False)

    n_cond = 1 + batch
    pad = (-n_cond) % V7X_SUBLANES
    cond = jnp.concatenate([c_ctx[None], c, jnp.zeros((pad, d), F32)], axis=0)
    mods_all = _ada(cond, w_mod, b_mod)

    pos = _grid_pos_embed(t_lat // GRID_W, d)
    xall = _addpos(lay, ctx.reshape(batch * t_ctx, d), x.reshape(batch * t_lat, d), pos)

    gd = d // FN_GROUPS
    cg, sg = _dft_tables(gd)
    cs_tab = jnp.asarray(np.concatenate([cg, sg], axis=1)).astype(BF16)
    ct_lat, st_lat = (jnp.asarray(a).astype(BF16) for a in _dft_tables(t_lat))
    ct_ctx, st_ctx = (jnp.asarray(a).astype(BF16) for a in _dft_tables(t_ctx))

    for i in range(depth):
        last = i == depth - 1
        j = i // N_MIXERS
        mods = mods_all[i].reshape(mods_all.shape[1], 1, N_MOD * d)
        g1n = norm1_g[i][None]
        g2n = norm2_g[i][None]
        if i % N_MIXERS == 0:
            cur = lay
            gate, rec = _rg1(cur, xall, g1n, mods, rg_w_in[j].astype(BF16))
            hs = []
            for dr in range(2):
                hs.append(_rg2(cur, rec, rg_conv_w[j], rg_conv_b[j][None],
                               rg_w_a[j, dr].astype(BF16), rg_w_i[j, dr].astype(BF16),
                               rg_b_a[j, dr][None], rg_b_i[j, dr][None], rg_lam[j, dr][None],
                               rev=dr == 1))
            xall = _proj_res(cur, [hs[0], hs[1], gate], xall, mods, rg_w_out[j].astype(BF16))
        else:
            cur = lay_lat if last else lay
            x_off = lay.nct if last else 0
            zc, zs = _f1(cur, xall, g1n, mods, cs_tab, x_off=x_off)
            fre = [_f2(zc, zs, ct_lat, st_lat, batch, t_lat, cur.nct * cur.R // t_lat)]
            if not last:
                fre = [_f2(zc, zs, ct_ctx, st_ctx, batch, t_ctx, 0)] + fre
            xall = _proj_res(cur, fre, xall, mods, fn_w_out[j].astype(BF16), x_off=x_off)
        if last and cur is lay:
            xall = xall[lay.nct * lay.R:]
            cur = lay_lat
        xall = _moe(cur, xall, g2n, mods, router_w[i].T, router_b[i][:, None],
                    exp_w_gate[i], exp_w_up[i], exp_w_down[i], sh_w_gate[i], sh_w_up[i], sh_w_down[i])

    out = _final_norm(xall, final_norm_g[None], lay_lat.R)
    return out.reshape(batch, t_lat, d)
```

```python
import functools
import math

import numpy as np
import jax
import jax.numpy as jnp
from jax import lax
from jax.experimental import pallas as pl
from jax.experimental.pallas import tpu as pltpu

GRID_W = 64
N_MIXERS = 2
LRU_HEADS = 8
CONV_W = 4
CONV_LEFT = 2
RG_C = 8.0
FN_GROUPS = 8
TOP_K = 8
N_GROUPS = 8
TOPK_GROUPS = 4
ROUTED_SCALE = 2.5
N_MOD = 6
EPS = 1e-6
POS_THETA = 10000.0

V7X_VMEM_BYTES = 64 * 1024 * 1024
V7X_SUBLANES = 8
V7X_BF16_ROWS = 16
V7X_MXU_DIM = 256
MAX_ROW_TILE = 256
VMEM_LIMIT = 56 * 1024 * 1024

F32 = jnp.float32
BF16 = jnp.bfloat16


def _params(sem, vmem=VMEM_LIMIT):
    return pltpu.CompilerParams(dimension_semantics=sem, vmem_limit_bytes=vmem)


def _resident(shape, index_map):
    return pl.BlockSpec(shape, index_map, pipeline_mode=pl.Buffered(1))


def _col_chunk(n, cap=512):
    c = min(n, cap)
    while n % c:
        c //= 2
    return c


def _normmod(x, g, shift, scale):
    ms = jnp.mean(x * x, axis=-1, keepdims=True)
    y = x * lax.rsqrt(ms + EPS) * g
    return y * (1.0 + scale) + shift


def _nt_dot(a, b):
    return lax.dot_general(a, b, (((1,), (1,)), ((), ())), preferred_element_type=F32)


class _Layout:
    def __init__(self, batch, t_ctx, t_lat, with_ctx=True):
        assert batch % V7X_SUBLANES == 0
        g = math.gcd(t_ctx, t_lat)
        fits = [tt for tt in range(1, g + 1)
                if g % tt == 0 and tt * batch <= MAX_ROW_TILE and (tt * batch) % V7X_BF16_ROWS == 0]
        tt = max(fits)
        self.TT = tt
        self.R = tt * batch
        self.B = batch
        self.Tc = t_ctx
        self.T = t_lat
        self.nct = t_ctx // tt if with_ctx else 0
        self.nlt = t_lat // tt
        self.ntiles = self.nct + self.nlt
        self.N = self.ntiles * self.R


def _mod_spec(lay, d, which):
    return pl.BlockSpec((None, lay.R, d), lambda i, *_: (jnp.where(i < lay.nct, 0, 1), 0, which))


def _ada_kernel(cond_ref, w_ref, b_ref, out_ref):
    s = jax.nn.silu(cond_ref[...]).astype(BF16)
    out_ref[...] = jnp.dot(s, w_ref[...].astype(BF16), preferred_element_type=F32) + b_ref[...]


def _ada(cond, w_mod, b_mod):
    depth, d, nm = w_mod.shape
    rows = cond.shape[0]
    tn = _col_chunk(nm, 1024)
    return pl.pallas_call(
        _ada_kernel,
        grid=(depth, nm // tn),
        in_specs=[
            pl.BlockSpec((rows, d), lambda l, j: (0, 0)),
            pl.BlockSpec((None, d, tn), lambda l, j: (l, 0, j)),
            pl.BlockSpec((None, 1, tn), lambda l, j: (l, 0, j)),
        ],
        out_specs=pl.BlockSpec((None, rows, tn), lambda l, j: (l, 0, j)),
        out_shape=jax.ShapeDtypeStruct((depth, rows, nm), F32),
        compiler_params=_params(("arbitrary", "arbitrary")),
        name="ada_mod",
    )(cond, w_mod, b_mod.reshape(depth, 1, nm))


def _addpos_kernel(ctx_ref, x_ref, pos_ref, out_ref, *, nct, tt, batch):
    i = pl.program_id(0)

    @pl.when(i < nct)
    def _():
        out_ref[...] = ctx_ref[...]

    @pl.when(i >= nct)
    def _():
        for s in range(tt):
            rows = slice(s * batch, (s + 1) * batch)
            out_ref[rows, :] = x_ref[rows, :] + pos_ref[s:s + 1, :]


def _addpos(lay, ctx2d, x2d, pos):
    d = x2d.shape[1]
    r = lay.R
    return pl.pallas_call(
        functools.partial(_addpos_kernel, nct=lay.nct, tt=lay.TT, batch=lay.B),
        grid=(lay.ntiles,),
        in_specs=[
            pl.BlockSpec((r, d), lambda i: (jnp.minimum(i, lay.nct - 1), 0)),
            pl.BlockSpec((r, d), lambda i: (jnp.maximum(i - lay.nct, 0), 0)),
            pl.BlockSpec((lay.TT, d), lambda i: (jnp.maximum(i - lay.nct, 0), 0)),
        ],
        out_specs=pl.BlockSpec((r, d), lambda i: (i, 0)),
        out_shape=jax.ShapeDtypeStruct((lay.N, d), F32),
        compiler_params=_params(("arbitrary",)),
        name="add_pos",
    )(ctx2d, x2d, pos)


def _rg1_kernel(x_ref, g_ref, sh_ref, sc_ref, w_ref, gate_ref, rec_ref, *, width, cn):
    h = _normmod(x_ref[...], g_ref[...], sh_ref[...], sc_ref[...]).astype(BF16)
    for c in range(2 * width // cn):
        z = jnp.dot(h, w_ref[:, c * cn:(c + 1) * cn], preferred_element_type=F32)
        if c * cn < width:
            gate_ref[:, c * cn:(c + 1) * cn] = jax.nn.gelu(z).astype(BF16)
        else:
            rec_ref[:, c * cn - width:(c + 1) * cn - width] = z.astype(BF16)


def _rg1(lay, x, g, mods, w_in):
    d = x.shape[1]
    width = w_in.shape[1] // 2
    r = lay.R
    cn = _col_chunk(width)
    row = lambda i: (i, 0)
    return pl.pallas_call(
        functools.partial(_rg1_kernel, width=width, cn=cn),
        grid=(lay.ntiles,),
        in_specs=[
            pl.BlockSpec((r, d), row),
            _resident((1, d), lambda i: (0, 0)),
            _mod_spec(lay, d, 0),
            _mod_spec(lay, d, 1),
            _resident((d, 2 * width), lambda i: (0, 0)),
        ],
        out_specs=[pl.BlockSpec((r, width), row), pl.BlockSpec((r, width), row)],
        out_shape=[jax.ShapeDtypeStruct((lay.N, width), BF16)] * 2,
        compiler_params=_params(("arbitrary",)),
        name="rg_in_proj",
    )(x, g, mods, mods, w_in)


def _log_sigmoid(x):
    return jnp.minimum(x, 0.0) - jnp.log1p(jnp.exp(-jnp.abs(x)))


def _neg_expm1(t, exp_t):
    series = t * (1.0 + t * (0.5 + t * (1.0 / 6.0 + t * (1.0 / 24.0 + t * (1.0 / 120.0)))))
    return -jnp.where(t > -0.1, series, exp_t - 1.0)


def _rg2_kernel(cur_ref, prev_ref, next_ref, cw_ref, cb_ref, wa_ref, wi_ref, ba_ref, bi_ref,
                lam_ref, out_ref, carry_ref, *, rev, nct, nlt, tt, batch, hd, heads):
    s = pl.program_id(0)

    @pl.when(s == 0)
    def _():
        carry_ref[...] = jnp.zeros_like(carry_ref)

    in_ctx = s < nct
    pos = jnp.where(in_ctx, s, s - nct)
    n_seq = jnp.where(in_ctx, nct, nlt)
    chunk = (n_seq - 1 - pos) if rev else pos
    keep_prev = jnp.where(chunk == 0, 0.0, 1.0).astype(F32)
    keep_next = jnp.where(chunk == n_seq - 1, 0.0, 1.0).astype(F32)
    r = tt * batch

    for h in range(heads):
        sl = slice(h * hd, (h + 1) * hd)
        cur = cur_ref[:, sl].astype(F32)
        pv = prev_ref[:, sl].astype(F32) * keep_prev
        nx = next_ref[:, sl].astype(F32) * keep_next
        ext = jnp.concatenate([pv, cur, nx], axis=0)
        u = cb_ref[:, sl] + cw_ref[0:1, sl] * ext[0:r]
        for k in range(1, CONV_W):
            u = u + cw_ref[k:k + 1, sl] * ext[k * batch:k * batch + r]
        ub = u.astype(BF16)
        ra = jnp.dot(ub, wa_ref[h], preferred_element_type=F32) + ba_ref[:, sl]
        ri = jnp.dot(ub, wi_ref[h], preferred_element_type=F32) + bi_ref[:, sl]
        log_a = RG_C * jax.nn.sigmoid(ra) * _log_sigmoid(lam_ref[:, sl])
        a = jnp.exp(log_a)
        b = jnp.sqrt(_neg_expm1(2.0 * log_a, a * a)) * (jax.nn.sigmoid(ri) * u)

        state = carry_ref[:, sl]
        outs = [None] * tt
        for step in (range(tt - 1, -1, -1) if rev else range(tt)):
            rows = slice(step * batch, (step + 1) * batch)
            state = a[rows] * state + b[rows]
            outs[step] = state
        carry_ref[:, sl] = state
        out_ref[:, sl] = jnp.concatenate(outs, axis=0).astype(BF16)


def _rg2(lay, rec, conv_w, conv_b, w_a, w_i, b_a, b_i, lam, rev):
    n, width = rec.shape
    r, nct, nlt, batch = lay.R, lay.nct, lay.nlt, lay.B
    heads = w_a.shape[0]
    hd = width // heads
    left = CONV_LEFT * batch
    right = (CONV_W - 1 - CONV_LEFT) * batch
    assert r % left == 0 and r % right == 0

    def tile(s):
        if rev:
            return jnp.where(s < nct, nct - 1 - s, nct + nlt - 1 - (s - nct))
        return s

    const2 = lambda s: (0, 0)
    const3 = lambda s: (0, 0, 0)
    return pl.pallas_call(
        functools.partial(_rg2_kernel, rev=rev, nct=nct, nlt=nlt, tt=lay.TT, batch=batch, hd=hd, heads=heads),
        grid=(nct + nlt,),
        in_specs=[
            pl.BlockSpec((r, width), lambda s: (tile(s), 0)),
            pl.BlockSpec((left, width), lambda s: (jnp.maximum(tile(s) * (r // left) - 1, 0), 0)),
            pl.BlockSpec((right, width), lambda s: (jnp.minimum((tile(s) + 1) * (r // right), n // right - 1), 0)),
            _resident((CONV_W, width), const2),
            _resident((1, width), const2),
            _resident((heads, hd, hd), const3),
            _resident((heads, hd, hd), const3),
            _resident((1, width), const2),
            _resident((1, width), const2),
            _resident((1, width), const2),
        ],
        out_specs=pl.BlockSpec((r, width), lambda s: (tile(s), 0)),
        out_shape=jax.ShapeDtypeStruct((n, width), BF16),
        scratch_shapes=[pltpu.VMEM((batch, width), F32)],
        compiler_params=_params(("arbitrary",)),
        name="rg_scan_bwd" if rev else "rg_scan_fwd",
    )(rec, rec, rec, conv_w, conv_b, w_a, w_i, b_a, b_i, lam)


def _proj_res_kernel(*refs, mode, nct, cn):
    if mode == "gated":
        hf_ref, hb_ref, gate_ref, x_ref, g1_ref, w_ref, out_ref = refs
        a = ((hf_ref[...].astype(F32) + hb_ref[...].astype(F32)) * gate_ref[...].astype(F32)).astype(BF16)
    elif mode == "ctx_lat":
        ac_ref, al_ref, x_ref, g1_ref, w_ref, out_ref = refs
        a = jnp.where(pl.program_id(0) < nct, ac_ref[...], al_ref[...])
    else:
        a_ref, x_ref, g1_ref, w_ref, out_ref = refs
        a = a_ref[...]
    d = out_ref.shape[1]
    for c in range(d // cn):
        cs = slice(c * cn, (c + 1) * cn)
        y = jnp.dot(a, w_ref[:, cs], preferred_element_type=F32)
        out_ref[:, cs] = x_ref[:, cs] + g1_ref[:, cs] * y


def _proj_res(lay, a_list, x, mods, w_out, x_off=0):
    k, d = w_out.shape
    r = lay.R
    cn = _col_chunk(d)
    row = lambda i: (i, 0)
    mode = {1: "plain", 2: "ctx_lat", 3: "gated"}[len(a_list)]
    if mode == "ctx_lat":
        a_maps = [lambda i: (jnp.minimum(i, lay.nct - 1), 0), lambda i: (jnp.maximum(i - lay.nct, 0), 0)]
    else:
        a_maps = [row] * len(a_list)
    return pl.pallas_call(
        functools.partial(_proj_res_kernel, mode=mode, nct=lay.nct, cn=cn),
        grid=(lay.ntiles,),
        in_specs=[pl.BlockSpec((r, k), m) for m in a_maps] + [
            pl.BlockSpec((r, d), lambda i: (i + x_off, 0)),
            _mod_spec(lay, d, 2),
            _resident((k, d), lambda i: (0, 0)),
        ],
        out_specs=pl.BlockSpec((r, d), row),
        out_shape=jax.ShapeDtypeStruct((lay.N, d), F32),
        compiler_params=_params(("arbitrary",)),
        name="mixer_out_proj",
    )(*a_list, x, mods, w_out)


def _dft_tables(n):
    k = np.arange(n, dtype=np.int64)
    ang = 2.0 * np.pi * ((k[:, None] * k[None, :]) % n).astype(np.float64) / n
    s = 1.0 / np.sqrt(n)
    return (np.cos(ang) * s).astype(np.float32), (np.sin(ang) * s).astype(np.float32)


def _f1_kernel(x_ref, g_ref, sh_ref, sc_ref, cs_ref, zc_ref, zs_ref, *, groups, gd):
    h = _normmod(x_ref[...], g_ref[...], sh_ref[...], sc_ref[...]).astype(BF16)
    for gi in range(groups):
        sl = slice(gi * gd, (gi + 1) * gd)
        z = jnp.dot(h[:, sl], cs_ref[...], preferred_element_type=F32)
        zc_ref[:, sl] = z[:, :gd].astype(BF16)
        zs_ref[:, sl] = z[:, gd:].astype(BF16)


def _f1(r, x, g, mods, cs, first_tile, n_tiles, latent):
    d = x.shape[1]
    gd = d // FN_GROUPS
    row = lambda i: (i, 0)
    mod = lambda which: pl.BlockSpec((None, r, d), lambda i: (int(latent), 0, which))
    return pl.pallas_call(
        functools.partial(_f1_kernel, groups=FN_GROUPS, gd=gd),
        grid=(n_tiles,),
        in_specs=[
            pl.BlockSpec((r, d), lambda i: (i + first_tile, 0)),
            _resident((1, d), lambda i: (0, 0)),
            mod(0),
            mod(1),
            _resident((gd, 2 * gd), lambda i: (0, 0)),
        ],
        out_specs=[pl.BlockSpec((r, d), row), pl.BlockSpec((r, d), row)],
        out_shape=[jax.ShapeDtypeStruct((n_tiles * r, d), BF16)] * 2,
        compiler_params=_params(("arbitrary",)),
        name="fourier_channel_dft",
    )(x, g, mods, mods, cs)


def _f2_kernel(ct_ref, st_ref, zc_ref, zs_ref, out_ref, *, t, rc):
    for c in range(t // rc):
        rs = slice(c * rc, (c + 1) * rc)
        y = jnp.dot(ct_ref[rs, :], zc_ref[...], preferred_element_type=F32)
        y = y - jnp.dot(st_ref[rs, :], zs_ref[...], preferred_element_type=F32)
        out_ref[rs, :] = y.astype(BF16)


def _f2(zc, zs, ct, st, t):
    n, d = zc.shape
    cols = (n // t) * d
    tn = _col_chunk(cols)
    rc = _col_chunk(t)
    zmap = lambda j: (0, j)
    out = pl.pallas_call(
        functools.partial(_f2_kernel, t=t, rc=rc),
        grid=(cols // tn,),
        in_specs=[
            _resident((t, t), lambda j: (0, 0)),
            _resident((t, t), lambda j: (0, 0)),
            pl.BlockSpec((t, tn), zmap),
            pl.BlockSpec((t, tn), zmap),
        ],
        out_specs=pl.BlockSpec((t, tn), zmap),
        out_shape=jax.ShapeDtypeStruct((t, cols), BF16),
        compiler_params=_params(("arbitrary",)),
        name="fourier_time_dft",
    )(ct, st, zc.reshape(t, cols), zs.reshape(t, cols))
    return out.reshape(n, d)


def _pack_pairs(v):
    half = v.shape[1] // 2
    bits = pltpu.bitcast(v.astype(BF16).astype(F32), jnp.uint32)
    return (bits[:, :half] >> 16) | (bits[:, half:] & jnp.uint32(0xFFFF0000))


def _unpack_pairs(w):
    lo = pltpu.bitcast(w << 16, F32)
    hi = pltpu.bitcast(w & jnp.uint32(0xFFFF0000), F32)
    return lo, hi


def _router_kernel(x_ref, g_ref, sh_ref, sc_ref, rw_ref, rb_ref, tri_e_ref, tri_t_ref,
                   f_ref, ek_ref, rk_ref, wk_ref, cnt_ref, base_ref, *, n_exp):
    @pl.when(pl.program_id(0) == 0)
    def _():
        base_ref[...] = jnp.zeros_like(base_ref)

    f = _normmod(x_ref[...], g_ref[...], sh_ref[...], sc_ref[...])
    f_hi = f.astype(BF16)
    f_lo = (f - f_hi.astype(F32)).astype(BF16)
    f_ref[...] = _pack_pairs(f)
    rw = rw_ref[...]
    rw_hi = rw.astype(BF16)
    rw_lo = (rw - rw_hi.astype(F32)).astype(BF16)
    logits = _nt_dot(rw_hi, f_hi) + (_nt_dot(rw_hi, f_lo) + _nt_dot(rw_lo, f_hi))
    scores = jax.nn.sigmoid(logits)
    biased = scores + rb_ref[...]
    r = scores.shape[1]
    gsz = n_exp // N_GROUPS
    sub = lax.broadcasted_iota(jnp.int32, (gsz, r), 0)
    neg = jnp.float32(-jnp.inf)

    blocks = [biased[gi * gsz:(gi + 1) * gsz] for gi in range(N_GROUPS)]
    gscore = []
    for blk in blocks:
        m1 = jnp.max(blk, axis=0, keepdims=True)
        first = jnp.min(jnp.where(blk == m1, sub, gsz), axis=0, keepdims=True)
        m2 = jnp.max(jnp.where(sub == first, neg, blk), axis=0, keepdims=True)
        gscore.append(m1 + m2)
    masked = []
    for gi in range(N_GROUPS):
        rank = jnp.zeros((1, r), jnp.int32)
        for gj in range(N_GROUPS):
            if gj == gi:
                continue
            ahead = (gscore[gj] >= gscore[gi]) if gj < gi else (gscore[gj] > gscore[gi])
            rank = rank + ahead.astype(jnp.int32)
        masked.append(jnp.where(rank < TOPK_GROUPS, blocks[gi], neg))
    sel_f, sel_w = [], []
    for gi in range(N_GROUPS):
        blk = masked[gi]
        rank = jnp.zeros((gsz, r), jnp.int32)
        for gj in range(N_GROUPS):
            for j in range(gsz):
                other = masked[gj][j:j + 1]
                if gj < gi:
                    ahead = other >= blk
                elif gj > gi:
                    ahead = other > blk
                else:
                    ahead = (other > blk) | ((other == blk) & (sub > j))
                rank = rank + ahead.astype(jnp.int32)
        sel_f.append(jnp.where(rank < TOP_K, 1.0, 0.0))
        sel_w.append(jnp.where(rank < TOP_K, scores[gi * gsz:(gi + 1) * gsz], 0.0))
    tot = sel_w[0]
    for gi in range(1, N_GROUPS):
        tot = tot + sel_w[gi]
    denom = jnp.sum(tot, axis=0, keepdims=True)
    gates = jnp.concatenate(sel_w, axis=0) / denom * ROUTED_SCALE
    sel = jnp.concatenate(sel_f, axis=0)
    sel_b = sel.astype(BF16)
    slot = jnp.dot(tri_e_ref[...], sel_b, preferred_element_type=F32)
    base = base_ref[...]
    pos_in_expert = base + jnp.dot(sel_b, tri_t_ref[...], preferred_element_type=F32)
    base = base + jnp.sum(sel, axis=1, keepdims=True)
    base_ref[...] = base
    cnt_ref[...] = base
    e_idx = lax.broadcasted_iota(jnp.int32, sel.shape, 0).astype(F32)
    for k in range(TOP_K):
        hit = (sel > 0.0) & (slot == float(k))
        ek_ref[k:k + 1, :] = jnp.sum(jnp.where(hit, e_idx, 0.0), axis=0, keepdims=True).astype(jnp.int32)
        rk_ref[k:k + 1, :] = jnp.sum(jnp.where(hit, pos_in_expert, 0.0), axis=0, keepdims=True).astype(jnp.int32)
        wk_ref[k:k + 1, :] = jnp.sum(jnp.where(hit, gates, 0.0), axis=0, keepdims=True)


def _router(lay, x, g, mods, rw_t, rb):
    d = x.shape[1]
    n_exp = rw_t.shape[0]
    r = lay.R
    row = lambda i: (i, 0)
    tok = lambda i: (0, i)
    const = lambda i: (0, 0)
    tri_e = jnp.asarray(np.tril(np.ones((n_exp, n_exp), np.float32), -1)).astype(BF16)
    tri_t = jnp.asarray(np.triu(np.ones((r, r), np.float32), 1)).astype(BF16)
    return pl.pallas_call(
        functools.partial(_router_kernel, n_exp=n_exp),
        grid=(lay.ntiles,),
        in_specs=[
            pl.BlockSpec((r, d), row),
            _resident((1, d), const),
            _mod_spec(lay, d, 3),
            _mod_spec(lay, d, 4),
            _resident((n_exp, d), const),
            _resident((n_exp, 1), const),
            _resident((n_exp, n_exp), const),
            _resident((r, r), const),
        ],
        out_specs=[
            pl.BlockSpec((r, d // 2), row),
            pl.BlockSpec((TOP_K, r), tok),
            pl.BlockSpec((TOP_K, r), tok),
            pl.BlockSpec((TOP_K, r), tok),
            pl.BlockSpec((n_exp, 1), const),
        ],
        out_shape=[
            jax.ShapeDtypeStruct((lay.N, d // 2), jnp.uint32),
            jax.ShapeDtypeStruct((TOP_K, lay.N), jnp.int32),
            jax.ShapeDtypeStruct((TOP_K, lay.N), jnp.int32),
            jax.ShapeDtypeStruct((TOP_K, lay.N), F32),
            jax.ShapeDtypeStruct((n_exp, 1), F32),
        ],
        scratch_shapes=[pltpu.VMEM((n_exp, 1), F32)],
        compiler_params=_params(("arbitrary",)),
        name="moe_router",
    )(x, g, mods, mods, rw_t, rb, tri_e, tri_t)


def _dispatch_kernel(pad_start_ref, pad_len_ref, nused_ref, p_ref, f_ref, xs_ref, sem, zeros_ref,
                     *, n_exp, r, tmx):
    i = pl.program_id(0)

    def scatter_rows(t, carry):
        for k in range(TOP_K):
            p = p_ref[t * TOP_K + k]
            pltpu.make_async_copy(f_ref.at[pl.ds(t, 1)], xs_ref.at[pl.ds(p, 1)], sem.at[0]).start()
        return carry

    lax.fori_loop(0, r, scatter_rows, 0)
    pltpu.make_async_copy(xs_ref.at[pl.ds(0, r * TOP_K)], xs_ref.at[pl.ds(0, r * TOP_K)], sem.at[0]).wait()

    @pl.when(i == pl.num_programs(0) - 1)
    def _():
        zeros_ref[...] = jnp.zeros_like(zeros_ref)

        def pad_copies(e, act):
            start = pad_start_ref[e]
            length = pad_len_ref[e]
            head = jnp.minimum(length, (-start) & (V7X_SUBLANES - 1))
            base = start + head

            def row(j, c):
                act(pltpu.make_async_copy(zeros_ref.at[pl.ds(0, 1)], xs_ref.at[pl.ds(start + j, 1)], sem.at[1]))
                return c

            def group(j, c):
                off = pl.multiple_of(base + j * V7X_SUBLANES, V7X_SUBLANES)
                act(pltpu.make_async_copy(zeros_ref.at[pl.ds(0, V7X_SUBLANES)],
                                          xs_ref.at[pl.ds(off, V7X_SUBLANES)], sem.at[2]))
                return c

            lax.fori_loop(0, head, row, 0)
            lax.fori_loop(0, (length - head) // V7X_SUBLANES, group, 0)

        def tile_copy(m, act):
            dst = xs_ref.at[pl.ds(pl.multiple_of(m * tmx, tmx), tmx)]
            act(pltpu.make_async_copy(zeros_ref, dst, sem.at[3]))

        n_tiles = xs_ref.shape[0] // tmx
        for act in (lambda cp: cp.start(), lambda cp: cp.wait()):
            lax.fori_loop(0, n_exp, lambda e, c, act=act: (pad_copies(e, act), c)[1], 0)
            lax.fori_loop(nused_ref[0], n_tiles, lambda m, c, act=act: (tile_copy(m, act), c)[1], 0)


def _dispatch(lay, fpk, p_flat, pad_start, pad_len, n_used, n_rows, tmx):
    n, half = fpk.shape
    r = lay.R
    n_exp = pad_start.shape[0]
    grid_spec = pltpu.PrefetchScalarGridSpec(
        num_scalar_prefetch=3,
        grid=(lay.ntiles,),
        in_specs=[
            pl.BlockSpec((r * TOP_K,), lambda i, *_: (i,), memory_space=pltpu.SMEM),
            pl.BlockSpec((r, half), lambda i, *_: (i, 0)),
        ],
        out_specs=pl.BlockSpec(memory_space=pl.ANY),
        scratch_shapes=[pltpu.SemaphoreType.DMA((4,)), pltpu.VMEM((tmx, half), jnp.uint32)],
    )
    return pl.pallas_call(
        functools.partial(_dispatch_kernel, n_exp=n_exp, r=r, tmx=tmx),
        grid_spec=grid_spec,
        out_shape=jax.ShapeDtypeStruct((n_rows, half), jnp.uint32),
        compiler_params=_params(("arbitrary",)),
        name="moe_dispatch",
    )(pad_start, pad_len, n_used, p_flat, fpk)


def _swiglu_packed(xw, wg_ref, wu_ref, wd_ref):
    half = xw.shape[1]
    lo, hi = _unpack_pairs(xw)
    lo = lo.astype(BF16)
    hi = hi.astype(BF16)

    def proj(w_ref):
        top = jnp.dot(lo, w_ref[:half, :], preferred_element_type=F32)
        return top + jnp.dot(hi, w_ref[half:, :], preferred_element_type=F32)

    hid = (jax.nn.silu(proj(wg_ref)) * proj(wu_ref)).astype(BF16)
    return jnp.dot(hid, wd_ref[...], preferred_element_type=F32)


def _grouped_kernel(te_ref, nused_ref, xs_ref, wg_ref, wu_ref, wd_ref, ys_ref, wg_b, wu_b, wd_b):
    m = pl.program_id(0)

    @pl.when((m == 0) | (te_ref[m] != te_ref[jnp.maximum(m - 1, 0)]))
    def _():
        wg_b[...] = wg_ref[...].astype(BF16)
        wu_b[...] = wu_ref[...].astype(BF16)
        wd_b[...] = wd_ref[...].astype(BF16)

    @pl.when(m < nused_ref[0])
    def _():
        ys_ref[...] = _pack_pairs(_swiglu_packed(xs_ref[...], wg_b, wu_b, wd_b))

    @pl.when(m >= nused_ref[0])
    def _():
        ys_ref[...] = jnp.zeros_like(ys_ref)


def _grouped(xs, wg, wu, wd, layer, tile_expert, n_used, tmx):
    n_rows, half = xs.shape
    _, _, d, fdim = wg.shape
    grid_spec = pltpu.PrefetchScalarGridSpec(
        num_scalar_prefetch=2,
        grid=(n_rows // tmx,),
        in_specs=[
            pl.BlockSpec((tmx, half), lambda m, te, nu: (jnp.minimum(m, nu[0] - 1), 0)),
            pl.BlockSpec((None, None, d, fdim), lambda m, te, nu: (layer, te[m], 0, 0)),
            pl.BlockSpec((None, None, d, fdim), lambda m, te, nu: (layer, te[m], 0, 0)),
            pl.BlockSpec((None, None, fdim, d), lambda m, te, nu: (layer, te[m], 0, 0)),
        ],
        out_specs=pl.BlockSpec((tmx, half), lambda m, te, nu: (m, 0)),
        scratch_shapes=[pltpu.VMEM((d, fdim), BF16), pltpu.VMEM((d, fdim), BF16), pltpu.VMEM((fdim, d), BF16)],
    )
    return pl.pallas_call(
        _grouped_kernel,
        grid_spec=grid_spec,
        out_shape=jax.ShapeDtypeStruct((n_rows, half), jnp.uint32),
        compiler_params=_params(("arbitrary",)),
        name="moe_experts",
    )(tile_expert, n_used, xs, wg, wu, wd)


def _combine_kernel(p0_ref, pn_ref, w_ref, f_ref, x_ref, g2_ref, swg_ref, swu_ref, swd_ref, ys_ref, out_ref,
                    buf_a, buf_b, sh_ref, sem, *, r):
    i = pl.program_id(0)
    last = pl.num_programs(0) - 1
    half = f_ref.shape[1]
    rows8 = V7X_SUBLANES

    def issue(p_ref, dst, dsem, t0, count):
        for j in range(count):
            t = t0 + j
            for k in range(TOP_K):
                p = p_ref[t * TOP_K + k]
                pltpu.make_async_copy(ys_ref.at[pl.ds(p, 1)], dst.at[k, pl.ds(t, 1)], dsem).start()

    @pl.when(i == 0)
    def _():
        lax.fori_loop(0, r, lambda t, c: (issue(p0_ref, buf_a, sem.at[0], t, 1), c)[1], 0)

    def run(cur, cur_sem, nxt, nxt_sem):
        pltpu.make_async_copy(cur, cur, cur_sem).wait()
        sh_ref[...] = _swiglu_packed(f_ref[...], swg_ref, swu_ref, swd_ref)

        def finish_rows(g):
            rows = pl.ds(pl.multiple_of(g * rows8, rows8), rows8)
            acc_lo = sh_ref[rows, :half]
            acc_hi = sh_ref[rows, half:]
            w = w_ref[rows, :]
            for k in range(TOP_K):
                lo, hi = _unpack_pairs(cur[k, rows, :])
                wcol = w[:, k:k + 1]
                acc_lo = acc_lo + wcol * lo
                acc_hi = acc_hi + wcol * hi
            out_ref[rows, :half] = x_ref[rows, :half] + g2_ref[rows, :half] * acc_lo
            out_ref[rows, half:] = x_ref[rows, half:] + g2_ref[rows, half:] * acc_hi

        @pl.when(i < last)
        def _():
            def body(g, c):
                issue(pn_ref, nxt, nxt_sem, pl.multiple_of(g * rows8, rows8), rows8)
                finish_rows(g)
                return c

            lax.fori_loop(0, r // rows8, body, 0)

        @pl.when(i == last)
        def _():
            lax.fori_loop(0, r // rows8, lambda g, c: (finish_rows(g), c)[1], 0)

    @pl.when(i % 2 == 0)
    def _():
        run(buf_a, sem.at[0], buf_b, sem.at[1])

    @pl.when(i % 2 == 1)
    def _():
        run(buf_b, sem.at[1], buf_a, sem.at[0])


def _combine(lay, ys, p_flat, w_tok, fpk, x, mods, swg, swu, swd):
    n, d = x.shape
    half = d // 2
    r = lay.R
    fdim = swd.shape[0]
    row = lambda i: (i, 0)
    const = lambda i: (0, 0)
    gather_buf = pltpu.VMEM((TOP_K, r, half), jnp.uint32)
    return pl.pallas_call(
        functools.partial(_combine_kernel, r=r),
        grid=(lay.ntiles,),
        in_specs=[
            pl.BlockSpec((r * TOP_K,), lambda i: (i,), memory_space=pltpu.SMEM),
            pl.BlockSpec((r * TOP_K,), lambda i: (jnp.minimum(i + 1, lay.ntiles - 1),), memory_space=pltpu.SMEM),
            pl.BlockSpec((r, TOP_K), row),
            pl.BlockSpec((r, half), row),
            pl.BlockSpec((r, d), row),
            _mod_spec(lay, d, 5),
            _resident((d, fdim), const),
            _resident((d, fdim), const),
            _resident((fdim, d), const),
            pl.BlockSpec(memory_space=pl.ANY),
        ],
        out_specs=pl.BlockSpec((r, d), row),
        out_shape=jax.ShapeDtypeStruct((n, d), F32),
        scratch_shapes=[gather_buf, gather_buf, pltpu.VMEM((r, d), F32), pltpu.SemaphoreType.DMA((2,))],
        compiler_params=_params(("arbitrary",)),
        name="moe_combine",
    )(p_flat, p_flat, w_tok, fpk, x, mods, swg, swu, swd, ys)


def _moe(lay, x, g2n, mods, rw_t, rb, wg, wu, wd, layer, swg, swu, swd):
    n_exp = rw_t.shape[0]
    tmx = 2 * lay.R
    fpk, ek, rk, wk, cnt = _router(lay, x, g2n, mods, rw_t, rb)
    cnt = cnt[:, 0].astype(jnp.int32)
    padded = (cnt + tmx - 1) // tmx * tmx
    offs = jnp.concatenate([jnp.zeros((1,), jnp.int32), jnp.cumsum(padded)])
    n_tiles = lay.N * TOP_K // tmx + n_exp
    e_ids = jnp.arange(n_exp, dtype=jnp.int32)[:, None, None]
    seg_start = jnp.sum(jnp.where(ek[None] > e_ids, padded[:, None, None], 0), axis=0)
    p_flat = (seg_start + rk).T.reshape(-1)
    w_tok = wk.T
    n_used = (offs[n_exp] // tmx).reshape(1)
    tile_start = jnp.minimum(jnp.arange(n_tiles, dtype=jnp.int32), n_used[0] - 1) * tmx
    tile_expert = jnp.sum((tile_start[:, None] >= offs[None, 1:]).astype(jnp.int32), axis=1)
    xs = _dispatch(lay, fpk, p_flat, offs[:-1] + cnt, padded - cnt, n_used, n_tiles * tmx, tmx)
    ys = _grouped(xs, wg, wu, wd, layer, tile_expert, n_used, tmx)
    return _combine(lay, ys, p_flat, w_tok, fpk, x, mods, swg.astype(BF16), swu.astype(BF16), swd.astype(BF16))


def _final_norm_kernel(x_ref, g_ref, out_ref):
    x = x_ref[...]
    ms = jnp.mean(x * x, axis=-1, keepdims=True)
    out_ref[...] = x * lax.rsqrt(ms + EPS) * g_ref[...]


def _final_norm(x, g, r):
    n, d = x.shape
    return pl.pallas_call(
        _final_norm_kernel,
        grid=(n // r,),
        in_specs=[pl.BlockSpec((r, d), lambda i: (i, 0)), _resident((1, d), lambda i: (0, 0))],
        out_specs=pl.BlockSpec((r, d), lambda i: (i, 0)),
        out_shape=jax.ShapeDtypeStruct((n, d), F32),
        compiler_params=_params(("arbitrary",)),
        name="final_norm",
    )(x, g)


def _grid_pos_embed(rows, dim):
    row = np.repeat(np.arange(rows, dtype=np.float32), GRID_W)
    col = np.tile(np.arange(GRID_W, dtype=np.float32), rows)
    quarter = dim // 4
    freqs = jnp.float32(POS_THETA) ** (-jnp.arange(quarter, dtype=F32) / quarter)
    ar = jnp.asarray(row)[:, None] * freqs
    ac = jnp.asarray(col)[:, None] * freqs
    return jnp.concatenate([jnp.sin(ar), jnp.cos(ar), jnp.sin(ac), jnp.cos(ac)], axis=-1).astype(F32)


def kernel(x, c, ctx, c_ctx, norm1_g, norm2_g, final_norm_g, w_mod, b_mod, rg_w_in, rg_conv_w, rg_conv_b, rg_w_a, rg_b_a, rg_w_i, rg_b_i, rg_lam, rg_w_out, fn_w_out, router_w, router_b, exp_w_gate, exp_w_up, exp_w_down, sh_w_gate, sh_w_up, sh_w_down):
    batch, t_lat, d = x.shape
    t_ctx = ctx.shape[1]
    depth = w_mod.shape[0]
    lay = _Layout(batch, t_ctx, t_lat, with_ctx=True)
    lay_lat = _Layout(batch, t_ctx, t_lat, with_ctx=False)

    n_cond = 1 + batch
    pad = (-n_cond) % V7X_SUBLANES
    cond = jnp.concatenate([c_ctx[None], c, jnp.zeros((pad, d), F32)], axis=0)
    mods_all = _ada(cond, w_mod, b_mod)

    pos = _grid_pos_embed(t_lat // GRID_W, d)
    ctx_tm = jnp.swapaxes(ctx, 0, 1).reshape(t_ctx * batch, d)
    x_tm = jnp.swapaxes(x, 0, 1).reshape(t_lat * batch, d)
    xall = _addpos(lay, ctx_tm, x_tm, pos)

    gd = d // FN_GROUPS
    cg, sg = _dft_tables(gd)
    cs_tab = jnp.asarray(np.concatenate([cg, sg], axis=1)).astype(BF16)
    ct_lat, st_lat = (jnp.asarray(a).astype(BF16) for a in _dft_tables(t_lat))
    ct_ctx, st_ctx = (jnp.asarray(a).astype(BF16) for a in _dft_tables(t_ctx))

    for i in range(depth):
        last = i == depth - 1
        j = i // N_MIXERS
        table = mods_all[i]
        mods = jnp.stack([jnp.broadcast_to(table[0:1], (lay.R, N_MOD * d)),
                          jnp.tile(table[1:1 + batch], (lay.TT, 1))])
        g1n = norm1_g[i][None]
        g2n = norm2_g[i][None]
        if i % N_MIXERS == 0:
            cur = lay
            gate, rec = _rg1(cur, xall, g1n, mods, rg_w_in[j].astype(BF16))
            hs = []
            for dr in range(2):
                hs.append(_rg2(cur, rec, rg_conv_w[j], rg_conv_b[j][None],
                               rg_w_a[j, dr].astype(BF16), rg_w_i[j, dr].astype(BF16),
                               rg_b_a[j, dr][None], rg_b_i[j, dr][None], rg_lam[j, dr][None],
                               rev=dr == 1))
            xall = _proj_res(cur, [hs[0], hs[1], gate], xall, mods, rg_w_out[j].astype(BF16))
        else:
            cur = lay_lat if last else lay
            x_off = lay.nct if last else 0
            zc, zs = _f1(lay.R, xall, g1n, mods, cs_tab, lay.nct, lay.nlt, latent=True)
            fre = [_f2(zc, zs, ct_lat, st_lat, t_lat)]
            if not last:
                zc, zs = _f1(lay.R, xall, g1n, mods, cs_tab, 0, lay.nct, latent=False)
                fre = [_f2(zc, zs, ct_ctx, st_ctx, t_ctx)] + fre
            xall = _proj_res(cur, fre, xall, mods, fn_w_out[j].astype(BF16), x_off=x_off)
        if last and cur is lay:
            xall = xall[lay.nct * lay.R:]
            cur = lay_lat
        xall = _moe(cur, xall, g2n, mods, router_w[i].T, router_b[i][:, None],
                    exp_w_gate, exp_w_up, exp_w_down, i, sh_w_gate[i], sh_w_up[i], sh_w_down[i])

    out = _final_norm(xall, final_norm_g[None], lay_lat.R)
    return jnp.swapaxes(out.reshape(t_lat, batch, d), 0, 1)
```

```python
import functools
import math

import numpy as np
import jax
import jax.numpy as jnp
from jax import lax
from jax.experimental import pallas as pl
from jax.experimental.pallas import tpu as pltpu

GRID_W = 64
N_MIXERS = 2
LRU_HEADS = 8
CONV_W = 4
CONV_LEFT = 2
RG_C = 8.0
FN_GROUPS = 8
TOP_K = 8
N_GROUPS = 8
TOPK_GROUPS = 4
ROUTED_SCALE = 2.5
N_MOD = 6
EPS = 1e-6
POS_THETA = 10000.0

V7X_VMEM_BYTES = 64 * 1024 * 1024
V7X_SUBLANES = 8
V7X_LANES = 128
V7X_BF16_ROWS = 16
V7X_MXU_DIM = 256
MAX_ROW_TILE = 256
VMEM_LIMIT = 56 * 1024 * 1024

F32 = jnp.float32
BF16 = jnp.bfloat16


def _params(sem, vmem=VMEM_LIMIT):
    return pltpu.CompilerParams(dimension_semantics=sem, vmem_limit_bytes=vmem)


def _resident(shape, index_map):
    return pl.BlockSpec(shape, index_map, pipeline_mode=pl.Buffered(1))


def _col_chunk(n, cap=512):
    c = min(n, cap)
    while n % c:
        c //= 2
    return c


def _normmod(x, g, shift, scale):
    ms = jnp.mean(x * x, axis=-1, keepdims=True)
    y = x * lax.rsqrt(ms + EPS) * g
    return y * (1.0 + scale) + shift


def _nt_dot(a, b):
    return lax.dot_general(a, b, (((1,), (1,)), ((), ())), preferred_element_type=F32)


def _split_by_batch(tile, s3, batch):
    r, d = tile.shape
    tt = r // batch
    nch = d // V7X_LANES
    for c in range(nch):
        s3[c] = tile[:, c * V7X_LANES:(c + 1) * V7X_LANES]
    return [jnp.concatenate([s3[c, pl.ds(b, tt, stride=batch), :] for c in range(nch)], axis=1)
            for b in range(batch)]


def _merge_by_batch(pieces, s3):
    batch = len(pieces)
    tt, d = pieces[0].shape
    nch = d // V7X_LANES
    for b, piece in enumerate(pieces):
        for c in range(nch):
            s3[c, pl.ds(b, tt, stride=batch), :] = piece[:, c * V7X_LANES:(c + 1) * V7X_LANES]
    return jnp.concatenate([s3[c] for c in range(nch)], axis=1)


class _Layout:
    def __init__(self, batch, t_ctx, t_lat, with_ctx=True):
        assert batch % V7X_SUBLANES == 0
        g = math.gcd(t_ctx, t_lat)
        fits = [tt for tt in range(1, g + 1)
                if g % tt == 0 and tt * batch <= MAX_ROW_TILE and (tt * batch) % V7X_BF16_ROWS == 0]
        tt = max(fits)
        self.TT = tt
        self.R = tt * batch
        self.B = batch
        self.Tc = t_ctx
        self.T = t_lat
        self.nct = t_ctx // tt if with_ctx else 0
        self.nlt = t_lat // tt
        self.ntiles = self.nct + self.nlt
        self.N = self.ntiles * self.R


def _mod_spec(lay, d, which):
    return pl.BlockSpec((None, lay.R, d), lambda i, *_: (jnp.where(i < lay.nct, 0, 1), 0, which))


def _ada_kernel(cond_ref, w_ref, b_ref, out_ref):
    s = jax.nn.silu(cond_ref[...]).astype(BF16)
    out_ref[...] = jnp.dot(s, w_ref[...].astype(BF16), preferred_element_type=F32) + b_ref[...]


def _ada(cond, w_mod, b_mod):
    depth, d, nm = w_mod.shape
    rows = cond.shape[0]
    tn = _col_chunk(nm, 1024)
    return pl.pallas_call(
        _ada_kernel,
        grid=(depth, nm // tn),
        in_specs=[
            pl.BlockSpec((rows, d), lambda l, j: (0, 0)),
            pl.BlockSpec((None, d, tn), lambda l, j: (l, 0, j)),
            pl.BlockSpec((None, 1, tn), lambda l, j: (l, 0, j)),
        ],
        out_specs=pl.BlockSpec((None, rows, tn), lambda l, j: (l, 0, j)),
        out_shape=jax.ShapeDtypeStruct((depth, rows, nm), F32),
        compiler_params=_params(("arbitrary", "arbitrary")),
        name="ada_mod",
    )(cond, w_mod, b_mod.reshape(depth, 1, nm))


def _addpos_kernel(ctx_ref, x_ref, pos_ref, out_ref, s3, *, nct, batch):
    i = pl.program_id(0)

    @pl.when(i < nct)
    def _():
        out_ref[...] = _merge_by_batch([ctx_ref[b] for b in range(batch)], s3)

    @pl.when(i >= nct)
    def _():
        out_ref[...] = _merge_by_batch([x_ref[b] + pos_ref[...] for b in range(batch)], s3)


def _addpos(lay, ctx, x, pos):
    batch, _, d = x.shape
    r, tt = lay.R, lay.TT
    return pl.pallas_call(
        functools.partial(_addpos_kernel, nct=lay.nct, batch=batch),
        grid=(lay.ntiles,),
        in_specs=[
            pl.BlockSpec((batch, tt, d), lambda i: (0, jnp.minimum(i, lay.nct - 1), 0)),
            pl.BlockSpec((batch, tt, d), lambda i: (0, jnp.maximum(i - lay.nct, 0), 0)),
            pl.BlockSpec((tt, d), lambda i: (jnp.maximum(i - lay.nct, 0), 0)),
        ],
        out_specs=pl.BlockSpec((r, d), lambda i: (i, 0)),
        out_shape=jax.ShapeDtypeStruct((lay.N, d), F32),
        scratch_shapes=[pltpu.VMEM((d // V7X_LANES, r, V7X_LANES), F32)],
        compiler_params=_params(("arbitrary",)),
        name="add_pos",
    )(ctx, x, pos)


def _rg1_kernel(x_ref, g_ref, sh_ref, sc_ref, w_ref, gate_ref, rec_ref, *, width, cn):
    h = _normmod(x_ref[...], g_ref[...], sh_ref[...], sc_ref[...]).astype(BF16)
    for c in range(2 * width // cn):
        z = jnp.dot(h, w_ref[:, c * cn:(c + 1) * cn], preferred_element_type=F32)
        if c * cn < width:
            gate_ref[:, c * cn:(c + 1) * cn] = jax.nn.gelu(z).astype(BF16)
        else:
            rec_ref[:, c * cn - width:(c + 1) * cn - width] = z.astype(BF16)


def _rg1(lay, x, g, mods, w_in):
    d = x.shape[1]
    width = w_in.shape[1] // 2
    r = lay.R
    cn = _col_chunk(width)
    row = lambda i: (i, 0)
    return pl.pallas_call(
        functools.partial(_rg1_kernel, width=width, cn=cn),
        grid=(lay.ntiles,),
        in_specs=[
            pl.BlockSpec((r, d), row),
            _resident((1, d), lambda i: (0, 0)),
            _mod_spec(lay, d, 0),
            _mod_spec(lay, d, 1),
            _resident((d, 2 * width), lambda i: (0, 0)),
        ],
        out_specs=[pl.BlockSpec((r, width), row), pl.BlockSpec((r, width), row)],
        out_shape=[jax.ShapeDtypeStruct((lay.N, width), BF16)] * 2,
        compiler_params=_params(("arbitrary",)),
        name="rg_in_proj",
    )(x, g, mods, mods, w_in)


def _log_sigmoid(x):
    return jnp.minimum(x, 0.0) - jnp.log1p(jnp.exp(-jnp.abs(x)))


def _neg_expm1(t, exp_t):
    series = t * (1.0 + t * (0.5 + t * (1.0 / 6.0 + t * (1.0 / 24.0 + t * (1.0 / 120.0)))))
    return -jnp.where(t > -0.1, series, exp_t - 1.0)


def _rg2_kernel(cur_ref, prev_ref, next_ref, cw_ref, cb_ref, wa_ref, wi_ref, ba_ref, bi_ref,
                lam_ref, out_ref, carry_ref, *, rev, nct, nlt, tt, batch, hd, heads):
    s = pl.program_id(0)

    @pl.when(s == 0)
    def _():
        carry_ref[...] = jnp.zeros_like(carry_ref)

    in_ctx = s < nct
    pos = jnp.where(in_ctx, s, s - nct)
    n_seq = jnp.where(in_ctx, nct, nlt)
    chunk = (n_seq - 1 - pos) if rev else pos
    keep_prev = jnp.where(chunk == 0, 0.0, 1.0).astype(F32)
    keep_next = jnp.where(chunk == n_seq - 1, 0.0, 1.0).astype(F32)
    r = tt * batch

    for h in range(heads):
        sl = slice(h * hd, (h + 1) * hd)
        cur = cur_ref[:, sl].astype(F32)
        pv = prev_ref[:, sl].astype(F32) * keep_prev
        nx = next_ref[:, sl].astype(F32) * keep_next
        ext = jnp.concatenate([pv, cur, nx], axis=0)
        u = cb_ref[:, sl] + cw_ref[0:1, sl] * ext[0:r]
        for k in range(1, CONV_W):
            u = u + cw_ref[k:k + 1, sl] * ext[k * batch:k * batch + r]
        ub = u.astype(BF16)
        ra = jnp.dot(ub, wa_ref[h], preferred_element_type=F32) + ba_ref[:, sl]
        ri = jnp.dot(ub, wi_ref[h], preferred_element_type=F32) + bi_ref[:, sl]
        log_a = RG_C * jax.nn.sigmoid(ra) * _log_sigmoid(lam_ref[:, sl])
        a = jnp.exp(log_a)
        b = jnp.sqrt(_neg_expm1(2.0 * log_a, a * a)) * (jax.nn.sigmoid(ri) * u)

        state = carry_ref[:, sl]
        outs = [None] * tt
        for step in (range(tt - 1, -1, -1) if rev else range(tt)):
            rows = slice(step * batch, (step + 1) * batch)
            state = a[rows] * state + b[rows]
            outs[step] = state
        carry_ref[:, sl] = state
        out_ref[:, sl] = jnp.concatenate(outs, axis=0).astype(BF16)


def _rg2(lay, rec, conv_w, conv_b, w_a, w_i, b_a, b_i, lam, rev):
    n, width = rec.shape
    r, nct, nlt, batch = lay.R, lay.nct, lay.nlt, lay.B
    heads = w_a.shape[0]
    hd = width // heads
    left = CONV_LEFT * batch
    right = (CONV_W - 1 - CONV_LEFT) * batch
    assert r % left == 0 and r % right == 0

    def tile(s):
        if rev:
            return jnp.where(s < nct, nct - 1 - s, nct + nlt - 1 - (s - nct))
        return s

    const2 = lambda s: (0, 0)
    const3 = lambda s: (0, 0, 0)
    return pl.pallas_call(
        functools.partial(_rg2_kernel, rev=rev, nct=nct, nlt=nlt, tt=lay.TT, batch=batch, hd=hd, heads=heads),
        grid=(nct + nlt,),
        in_specs=[
            pl.BlockSpec((r, width), lambda s: (tile(s), 0)),
            pl.BlockSpec((left, width), lambda s: (jnp.maximum(tile(s) * (r // left) - 1, 0), 0)),
            pl.BlockSpec((right, width), lambda s: (jnp.minimum((tile(s) + 1) * (r // right), n // right - 1), 0)),
            _resident((CONV_W, width), const2),
            _resident((1, width), const2),
            _resident((heads, hd, hd), const3),
            _resident((heads, hd, hd), const3),
            _resident((1, width), const2),
            _resident((1, width), const2),
            _resident((1, width), const2),
        ],
        out_specs=pl.BlockSpec((r, width), lambda s: (tile(s), 0)),
        out_shape=jax.ShapeDtypeStruct((n, width), BF16),
        scratch_shapes=[pltpu.VMEM((batch, width), F32)],
        compiler_params=_params(("arbitrary",)),
        name="rg_scan_bwd" if rev else "rg_scan_fwd",
    )(rec, rec, rec, conv_w, conv_b, w_a, w_i, b_a, b_i, lam)


def _proj_res_kernel(*refs, mode, nct, cn, batch):
    if mode == "gated":
        hf_ref, hb_ref, gate_ref, x_ref, g1_ref, w_ref, out_ref = refs
        a = ((hf_ref[...].astype(F32) + hb_ref[...].astype(F32)) * gate_ref[...].astype(F32)).astype(BF16)
    else:
        if mode == "ctx_lat":
            ac_ref, al_ref, x_ref, g1_ref, w_ref, out_ref, a_s = refs
            blk = jnp.where(pl.program_id(0) < nct, ac_ref[...], al_ref[...])
        else:
            a_ref, x_ref, g1_ref, w_ref, out_ref, a_s = refs
            blk = a_ref[...]
        k = blk.shape[1] // batch
        a = _merge_by_batch([blk[:, b * k:(b + 1) * k].astype(F32) for b in range(batch)], a_s).astype(BF16)
    d = out_ref.shape[1]
    for c in range(d // cn):
        cs = slice(c * cn, (c + 1) * cn)
        y = jnp.dot(a, w_ref[:, cs], preferred_element_type=F32)
        out_ref[:, cs] = x_ref[:, cs] + g1_ref[:, cs] * y


def _proj_res(lay, a_list, x, mods, w_out, x_off=0):
    k, d = w_out.shape
    r = lay.R
    cn = _col_chunk(d)
    row = lambda i: (i, 0)
    mode = {1: "plain", 2: "ctx_lat", 3: "gated"}[len(a_list)]
    if mode == "ctx_lat":
        a_maps = [lambda i: (jnp.minimum(i, lay.nct - 1), 0), lambda i: (jnp.maximum(i - lay.nct, 0), 0)]
    else:
        a_maps = [row] * len(a_list)
    a_block = (r, k) if mode == "gated" else (lay.TT, lay.B * k)
    scratch = [] if mode == "gated" else [pltpu.VMEM((k // V7X_LANES, r, V7X_LANES), F32)]
    return pl.pallas_call(
        functools.partial(_proj_res_kernel, mode=mode, nct=lay.nct, cn=cn, batch=lay.B),
        grid=(lay.ntiles,),
        in_specs=[pl.BlockSpec(a_block, m) for m in a_maps] + [
            pl.BlockSpec((r, d), lambda i: (i + x_off, 0)),
            _mod_spec(lay, d, 2),
            _resident((k, d), lambda i: (0, 0)),
        ],
        out_specs=pl.BlockSpec((r, d), row),
        out_shape=jax.ShapeDtypeStruct((lay.N, d), F32),
        scratch_shapes=scratch,
        compiler_params=_params(("arbitrary",)),
        name="mixer_out_proj",
    )(*a_list, x, mods, w_out)


def _dft_tables(n):
    k = np.arange(n, dtype=np.int64)
    ang = 2.0 * np.pi * ((k[:, None] * k[None, :]) % n).astype(np.float64) / n
    s = 1.0 / np.sqrt(n)
    return (np.cos(ang) * s).astype(np.float32), (np.sin(ang) * s).astype(np.float32)


def _f1_kernel(x_ref, g_ref, sh_ref, sc_ref, cs_ref, zc_ref, zs_ref, s3, *, groups, gd, batch):
    h = _normmod(x_ref[...], g_ref[...], sh_ref[...], sc_ref[...]).astype(BF16)
    zs = [jnp.dot(h[:, gi * gd:(gi + 1) * gd], cs_ref[...], preferred_element_type=F32) for gi in range(groups)]
    d = groups * gd
    for out_ref, part in ((zc_ref, slice(0, gd)), (zs_ref, slice(gd, 2 * gd))):
        tile = jnp.concatenate([z[:, part] for z in zs], axis=1)
        for b, piece in enumerate(_split_by_batch(tile, s3, batch)):
            out_ref[:, b * d:(b + 1) * d] = piece.astype(BF16)


def _f1(lay, x, g, mods, cs, first_tile, n_tiles, latent):
    d = x.shape[1]
    r, tt, batch = lay.R, lay.TT, lay.B
    gd = d // FN_GROUPS
    row = lambda i: (i, 0)
    mod = lambda which: pl.BlockSpec((None, r, d), lambda i: (int(latent), 0, which))
    return pl.pallas_call(
        functools.partial(_f1_kernel, groups=FN_GROUPS, gd=gd, batch=batch),
        grid=(n_tiles,),
        in_specs=[
            pl.BlockSpec((r, d), lambda i: (i + first_tile, 0)),
            _resident((1, d), lambda i: (0, 0)),
            mod(0),
            mod(1),
            _resident((gd, 2 * gd), lambda i: (0, 0)),
        ],
        out_specs=[pl.BlockSpec((tt, batch * d), row), pl.BlockSpec((tt, batch * d), row)],
        out_shape=[jax.ShapeDtypeStruct((n_tiles * tt, batch * d), BF16)] * 2,
        scratch_shapes=[pltpu.VMEM((d // V7X_LANES, r, V7X_LANES), F32)],
        compiler_params=_params(("arbitrary",)),
        name="fourier_channel_dft",
    )(x, g, mods, mods, cs)


def _f2_kernel(ct_ref, st_ref, zc_ref, zs_ref, out_ref, *, t, rc):
    for c in range(t // rc):
        rs = slice(c * rc, (c + 1) * rc)
        y = jnp.dot(ct_ref[rs, :], zc_ref[...], preferred_element_type=F32)
        y = y - jnp.dot(st_ref[rs, :], zs_ref[...], preferred_element_type=F32)
        out_ref[rs, :] = y.astype(BF16)


def _f2(zc, zs, ct, st):
    t, cols = zc.shape
    tn = _col_chunk(cols)
    rc = _col_chunk(t)
    zmap = lambda j: (0, j)
    return pl.pallas_call(
        functools.partial(_f2_kernel, t=t, rc=rc),
        grid=(cols // tn,),
        in_specs=[
            _resident((t, t), lambda j: (0, 0)),
            _resident((t, t), lambda j: (0, 0)),
            pl.BlockSpec((t, tn), zmap),
            pl.BlockSpec((t, tn), zmap),
        ],
        out_specs=pl.BlockSpec((t, tn), zmap),
        out_shape=jax.ShapeDtypeStruct((t, cols), BF16),
        compiler_params=_params(("arbitrary",)),
        name="fourier_time_dft",
    )(ct, st, zc, zs)


def _pack_pairs(v):
    half = v.shape[1] // 2
    bits = pltpu.bitcast(v.astype(BF16).astype(F32), jnp.uint32)
    return (bits[:, :half] >> 16) | (bits[:, half:] & jnp.uint32(0xFFFF0000))


def _unpack_pairs(w):
    lo = pltpu.bitcast(w << 16, F32)
    hi = pltpu.bitcast(w & jnp.uint32(0xFFFF0000), F32)
    return lo, hi


def _router_kernel(x_ref, g_ref, sh_ref, sc_ref, rw_ref, rb_ref, tri_e_ref, tri_t_ref,
                   f_ref, ek_ref, rk_ref, wk_ref, cnt_ref, base_ref, *, n_exp):
    @pl.when(pl.program_id(0) == 0)
    def _():
        base_ref[...] = jnp.zeros_like(base_ref)

    f = _normmod(x_ref[...], g_ref[...], sh_ref[...], sc_ref[...])
    f_hi = f.astype(BF16)
    f_lo = (f - f_hi.astype(F32)).astype(BF16)
    f_ref[...] = _pack_pairs(f)
    rw = rw_ref[...]
    rw_hi = rw.astype(BF16)
    rw_lo = (rw - rw_hi.astype(F32)).astype(BF16)
    logits = _nt_dot(rw_hi, f_hi) + (_nt_dot(rw_hi, f_lo) + _nt_dot(rw_lo, f_hi))
    scores = jax.nn.sigmoid(logits)
    biased = scores + rb_ref[...]
    r = scores.shape[1]
    gsz = n_exp // N_GROUPS
    sub = lax.broadcasted_iota(jnp.int32, (gsz, r), 0)
    neg = jnp.float32(-jnp.inf)

    blocks = [biased[gi * gsz:(gi + 1) * gsz] for gi in range(N_GROUPS)]
    gscore = []
    for blk in blocks:
        m1 = jnp.max(blk, axis=0, keepdims=True)
        first = jnp.min(jnp.where(blk == m1, sub, gsz), axis=0, keepdims=True)
        m2 = jnp.max(jnp.where(sub == first, neg, blk), axis=0, keepdims=True)
        gscore.append(m1 + m2)
    masked = []
    for gi in range(N_GROUPS):
        rank = jnp.zeros((1, r), jnp.int32)
        for gj in range(N_GROUPS):
            if gj == gi:
                continue
            ahead = (gscore[gj] >= gscore[gi]) if gj < gi else (gscore[gj] > gscore[gi])
            rank = rank + ahead.astype(jnp.int32)
        masked.append(jnp.where(rank < TOPK_GROUPS, blocks[gi], neg))
    sel_f, sel_w = [], []
    for gi in range(N_GROUPS):
        blk = masked[gi]
        rank = jnp.zeros((gsz, r), jnp.int32)
        for gj in range(N_GROUPS):
            for j in range(gsz):
                other = masked[gj][j:j + 1]
                if gj < gi:
                    ahead = other >= blk
                elif gj > gi:
                    ahead = other > blk
                else:
                    ahead = (other > blk) | ((other == blk) & (sub > j))
                rank = rank + ahead.astype(jnp.int32)
        sel_f.append(jnp.where(rank < TOP_K, 1.0, 0.0))
        sel_w.append(jnp.where(rank < TOP_K, scores[gi * gsz:(gi + 1) * gsz], 0.0))
    tot = sel_w[0]
    for gi in range(1, N_GROUPS):
        tot = tot + sel_w[gi]
    denom = jnp.sum(tot, axis=0, keepdims=True)
    gates = jnp.concatenate(sel_w, axis=0) / denom * ROUTED_SCALE
    sel = jnp.concatenate(sel_f, axis=0)
    sel_b = sel.astype(BF16)
    slot = jnp.dot(tri_e_ref[...], sel_b, preferred_element_type=F32)
    base = base_ref[...]
    pos_in_expert = base + jnp.dot(sel_b, tri_t_ref[...], preferred_element_type=F32)
    base = base + jnp.sum(sel, axis=1, keepdims=True)
    base_ref[...] = base
    cnt_ref[...] = base
    e_idx = lax.broadcasted_iota(jnp.int32, sel.shape, 0).astype(F32)
    for k in range(TOP_K):
        hit = (sel > 0.0) & (slot == float(k))
        ek_ref[k:k + 1, :] = jnp.sum(jnp.where(hit, e_idx, 0.0), axis=0, keepdims=True).astype(jnp.int32)
        rk_ref[k:k + 1, :] = jnp.sum(jnp.where(hit, pos_in_expert, 0.0), axis=0, keepdims=True).astype(jnp.int32)
        wk_ref[k:k + 1, :] = jnp.sum(jnp.where(hit, gates, 0.0), axis=0, keepdims=True)


def _router(lay, x, g, mods, rw_t, rb):
    d = x.shape[1]
    n_exp = rw_t.shape[0]
    r = lay.R
    row = lambda i: (i, 0)
    tok = lambda i: (0, i)
    const = lambda i: (0, 0)
    tri_e = jnp.asarray(np.tril(np.ones((n_exp, n_exp), np.float32), -1)).astype(BF16)
    tri_t = jnp.asarray(np.triu(np.ones((r, r), np.float32), 1)).astype(BF16)
    return pl.pallas_call(
        functools.partial(_router_kernel, n_exp=n_exp),
        grid=(lay.ntiles,),
        in_specs=[
            pl.BlockSpec((r, d), row),
            _resident((1, d), const),
            _mod_spec(lay, d, 3),
            _mod_spec(lay, d, 4),
            _resident((n_exp, d), const),
            _resident((n_exp, 1), const),
            _resident((n_exp, n_exp), const),
            _resident((r, r), const),
        ],
        out_specs=[
            pl.BlockSpec((r, d // 2), row),
            pl.BlockSpec((TOP_K, r), tok),
            pl.BlockSpec((TOP_K, r), tok),
            pl.BlockSpec((TOP_K, r), tok),
            pl.BlockSpec((n_exp, 1), const),
        ],
        out_shape=[
            jax.ShapeDtypeStruct((lay.N, d // 2), jnp.uint32),
            jax.ShapeDtypeStruct((TOP_K, lay.N), jnp.int32),
            jax.ShapeDtypeStruct((TOP_K, lay.N), jnp.int32),
            jax.ShapeDtypeStruct((TOP_K, lay.N), F32),
            jax.ShapeDtypeStruct((n_exp, 1), F32),
        ],
        scratch_shapes=[pltpu.VMEM((n_exp, 1), F32)],
        compiler_params=_params(("arbitrary",)),
        name="moe_router",
    )(x, g, mods, mods, rw_t, rb, tri_e, tri_t)


def _dispatch_kernel(pad_start_ref, pad_len_ref, nused_ref, p_ref, f_ref, xs_ref, sem, zeros_ref,
                     *, n_exp, r, tmx):
    i = pl.program_id(0)

    def scatter_rows(t, carry):
        for k in range(TOP_K):
            p = p_ref[t * TOP_K + k]
            pltpu.make_async_copy(f_ref.at[pl.ds(t, 1)], xs_ref.at[pl.ds(p, 1)], sem.at[0]).start()
        return carry

    lax.fori_loop(0, r, scatter_rows, 0)
    pltpu.make_async_copy(xs_ref.at[pl.ds(0, r * TOP_K)], xs_ref.at[pl.ds(0, r * TOP_K)], sem.at[0]).wait()

    @pl.when(i == pl.num_programs(0) - 1)
    def _():
        zeros_ref[...] = jnp.zeros_like(zeros_ref)

        def pad_copies(e, act):
            start = pad_start_ref[e]
            length = pad_len_ref[e]
            head = jnp.minimum(length, (-start) & (V7X_SUBLANES - 1))
            base = start + head

            def row(j, c):
                act(pltpu.make_async_copy(zeros_ref.at[pl.ds(0, 1)], xs_ref.at[pl.ds(start + j, 1)], sem.at[1]))
                return c

            def group(j, c):
                off = pl.multiple_of(base + j * V7X_SUBLANES, V7X_SUBLANES)
                act(pltpu.make_async_copy(zeros_ref.at[pl.ds(0, V7X_SUBLANES)],
                                          xs_ref.at[pl.ds(off, V7X_SUBLANES)], sem.at[2]))
                return c

            lax.fori_loop(0, head, row, 0)
            lax.fori_loop(0, (length - head) // V7X_SUBLANES, group, 0)

        def tile_copy(m, act):
            dst = xs_ref.at[pl.ds(pl.multiple_of(m * tmx, tmx), tmx)]
            act(pltpu.make_async_copy(zeros_ref, dst, sem.at[3]))

        n_tiles = xs_ref.shape[0] // tmx
        for act in (lambda cp: cp.start(), lambda cp: cp.wait()):
            lax.fori_loop(0, n_exp, lambda e, c, act=act: (pad_copies(e, act), c)[1], 0)
            lax.fori_loop(nused_ref[0], n_tiles, lambda m, c, act=act: (tile_copy(m, act), c)[1], 0)


def _dispatch(lay, fpk, p_flat, pad_start, pad_len, n_used, n_rows, tmx):
    n, half = fpk.shape
    r = lay.R
    n_exp = pad_start.shape[0]
    grid_spec = pltpu.PrefetchScalarGridSpec(
        num_scalar_prefetch=3,
        grid=(lay.ntiles,),
        in_specs=[
            pl.BlockSpec((r * TOP_K,), lambda i, *_: (i,), memory_space=pltpu.SMEM),
            pl.BlockSpec((r, half), lambda i, *_: (i, 0)),
        ],
        out_specs=pl.BlockSpec(memory_space=pl.ANY),
        scratch_shapes=[pltpu.SemaphoreType.DMA((4,)), pltpu.VMEM((tmx, half), jnp.uint32)],
    )
    return pl.pallas_call(
        functools.partial(_dispatch_kernel, n_exp=n_exp, r=r, tmx=tmx),
        grid_spec=grid_spec,
        out_shape=jax.ShapeDtypeStruct((n_rows, half), jnp.uint32),
        compiler_params=_params(("arbitrary",)),
        name="moe_dispatch",
    )(pad_start, pad_len, n_used, p_flat, fpk)


def _swiglu_packed(xw, wg_ref, wu_ref, wd_ref):
    half = xw.shape[1]
    lo, hi = _unpack_pairs(xw)
    lo = lo.astype(BF16)
    hi = hi.astype(BF16)

    def proj(w_ref):
        top = jnp.dot(lo, w_ref[:half, :], preferred_element_type=F32)
        return top + jnp.dot(hi, w_ref[half:, :], preferred_element_type=F32)

    hid = (jax.nn.silu(proj(wg_ref)) * proj(wu_ref)).astype(BF16)
    return jnp.dot(hid, wd_ref[...], preferred_element_type=F32)


def _grouped_kernel(te_ref, nused_ref, xs_ref, wg_ref, wu_ref, wd_ref, ys_ref, wg_b, wu_b, wd_b):
    m = pl.program_id(0)

    @pl.when((m == 0) | (te_ref[m] != te_ref[jnp.maximum(m - 1, 0)]))
    def _():
        wg_b[...] = wg_ref[...].astype(BF16)
        wu_b[...] = wu_ref[...].astype(BF16)
        wd_b[...] = wd_ref[...].astype(BF16)

    @pl.when(m < nused_ref[0])
    def _():
        ys_ref[...] = _pack_pairs(_swiglu_packed(xs_ref[...], wg_b, wu_b, wd_b))

    @pl.when(m >= nused_ref[0])
    def _():
        ys_ref[...] = jnp.zeros_like(ys_ref)


def _grouped(xs, wg, wu, wd, layer, tile_expert, n_used, tmx):
    n_rows, half = xs.shape
    _, _, d, fdim = wg.shape
    grid_spec = pltpu.PrefetchScalarGridSpec(
        num_scalar_prefetch=2,
        grid=(n_rows // tmx,),
        in_specs=[
            pl.BlockSpec((tmx, half), lambda m, te, nu: (jnp.minimum(m, nu[0] - 1), 0)),
            pl.BlockSpec((None, None, d, fdim), lambda m, te, nu: (layer, te[m], 0, 0)),
            pl.BlockSpec((None, None, d, fdim), lambda m, te, nu: (layer, te[m], 0, 0)),
            pl.BlockSpec((None, None, fdim, d), lambda m, te, nu: (layer, te[m], 0, 0)),
        ],
        out_specs=pl.BlockSpec((tmx, half), lambda m, te, nu: (m, 0)),
        scratch_shapes=[pltpu.VMEM((d, fdim), BF16), pltpu.VMEM((d, fdim), BF16), pltpu.VMEM((fdim, d), BF16)],
    )
    return pl.pallas_call(
        _grouped_kernel,
        grid_spec=grid_spec,
        out_shape=jax.ShapeDtypeStruct((n_rows, half), jnp.uint32),
        compiler_params=_params(("arbitrary",)),
        name="moe_experts",
    )(tile_expert, n_used, xs, wg, wu, wd)


def _combine_kernel(p0_ref, pn_ref, w_ref, f_ref, x_ref, g2_ref, swg_ref, swu_ref, swd_ref, ys_ref, out_ref,
                    buf_a, buf_b, sh_ref, sem, *, r):
    i = pl.program_id(0)
    last = pl.num_programs(0) - 1
    half = f_ref.shape[1]
    rows8 = V7X_SUBLANES

    def issue(p_ref, dst, dsem, t0, count):
        for j in range(count):
            t = t0 + j
            for k in range(TOP_K):
                p = p_ref[t * TOP_K + k]
                pltpu.make_async_copy(ys_ref.at[pl.ds(p, 1)], dst.at[k, pl.ds(t, 1)], dsem).start()

    @pl.when(i == 0)
    def _():
        lax.fori_loop(0, r, lambda t, c: (issue(p0_ref, buf_a, sem.at[0], t, 1), c)[1], 0)

    def run(cur, cur_sem, nxt, nxt_sem):
        pltpu.make_async_copy(cur, cur, cur_sem).wait()
        sh_ref[...] = _swiglu_packed(f_ref[...], swg_ref, swu_ref, swd_ref)

        def finish_rows(g):
            rows = pl.ds(pl.multiple_of(g * rows8, rows8), rows8)
            acc_lo = sh_ref[rows, :half]
            acc_hi = sh_ref[rows, half:]
            w = w_ref[rows, :]
            for k in range(TOP_K):
                lo, hi = _unpack_pairs(cur[k, rows, :])
                wcol = w[:, k:k + 1]
                acc_lo = acc_lo + wcol * lo
                acc_hi = acc_hi + wcol * hi
            out_ref[rows, :half] = x_ref[rows, :half] + g2_ref[rows, :half] * acc_lo
            out_ref[rows, half:] = x_ref[rows, half:] + g2_ref[rows, half:] * acc_hi

        @pl.when(i < last)
        def _():
            def body(g, c):
                issue(pn_ref, nxt, nxt_sem, pl.multiple_of(g * rows8, rows8), rows8)
                finish_rows(g)
                return c

            lax.fori_loop(0, r // rows8, body, 0)

        @pl.when(i == last)
        def _():
            lax.fori_loop(0, r // rows8, lambda g, c: (finish_rows(g), c)[1], 0)

    @pl.when(i % 2 == 0)
    def _():
        run(buf_a, sem.at[0], buf_b, sem.at[1])

    @pl.when(i % 2 == 1)
    def _():
        run(buf_b, sem.at[1], buf_a, sem.at[0])


def _combine(lay, ys, p_flat, w_tok, fpk, x, mods, swg, swu, swd):
    n, d = x.shape
    half = d // 2
    r = lay.R
    fdim = swd.shape[0]
    row = lambda i: (i, 0)
    const = lambda i: (0, 0)
    gather_buf = pltpu.VMEM((TOP_K, r, half), jnp.uint32)
    return pl.pallas_call(
        functools.partial(_combine_kernel, r=r),
        grid=(lay.ntiles,),
        in_specs=[
            pl.BlockSpec((r * TOP_K,), lambda i: (i,), memory_space=pltpu.SMEM),
            pl.BlockSpec((r * TOP_K,), lambda i: (jnp.minimum(i + 1, lay.ntiles - 1),), memory_space=pltpu.SMEM),
            pl.BlockSpec((r, TOP_K), row),
            pl.BlockSpec((r, half), row),
            pl.BlockSpec((r, d), row),
            _mod_spec(lay, d, 5),
            _resident((d, fdim), const),
            _resident((d, fdim), const),
            _resident((fdim, d), const),
            pl.BlockSpec(memory_space=pl.ANY),
        ],
        out_specs=pl.BlockSpec((r, d), row),
        out_shape=jax.ShapeDtypeStruct((n, d), F32),
        scratch_shapes=[gather_buf, gather_buf, pltpu.VMEM((r, d), F32), pltpu.SemaphoreType.DMA((2,))],
        compiler_params=_params(("arbitrary",)),
        name="moe_combine",
    )(p_flat, p_flat, w_tok, fpk, x, mods, swg, swu, swd, ys)


def _moe(lay, x, g2n, mods, rw_t, rb, wg, wu, wd, layer, swg, swu, swd):
    n_exp = rw_t.shape[0]
    tmx = 2 * lay.R
    fpk, ek, rk, wk, cnt = _router(lay, x, g2n, mods, rw_t, rb)
    cnt = cnt[:, 0].astype(jnp.int32)
    padded = (cnt + tmx - 1) // tmx * tmx
    offs = jnp.concatenate([jnp.zeros((1,), jnp.int32), jnp.cumsum(padded)])
    n_tiles = lay.N * TOP_K // tmx + n_exp
    e_ids = jnp.arange(n_exp, dtype=jnp.int32)[:, None, None]
    seg_start = jnp.sum(jnp.where(ek[None] > e_ids, padded[:, None, None], 0), axis=0)
    p_flat = (seg_start + rk).T.reshape(-1)
    w_tok = wk.T
    n_used = (offs[n_exp] // tmx).reshape(1)
    tile_start = jnp.minimum(jnp.arange(n_tiles, dtype=jnp.int32), n_used[0] - 1) * tmx
    tile_expert = jnp.sum((tile_start[:, None] >= offs[None, 1:]).astype(jnp.int32), axis=1)
    xs = _dispatch(lay, fpk, p_flat, offs[:-1] + cnt, padded - cnt, n_used, n_tiles * tmx, tmx)
    ys = _grouped(xs, wg, wu, wd, layer, tile_expert, n_used, tmx)
    return _combine(lay, ys, p_flat, w_tok, fpk, x, mods, swg.astype(BF16), swu.astype(BF16), swd.astype(BF16))


def _final_norm_kernel(x_ref, g_ref, out_ref, s3, *, batch):
    x = x_ref[...]
    ms = jnp.mean(x * x, axis=-1, keepdims=True)
    y = x * lax.rsqrt(ms + EPS) * g_ref[...]
    for b, piece in enumerate(_split_by_batch(y, s3, batch)):
        out_ref[b] = piece


def _final_norm(lay, x, g):
    n, d = x.shape
    r, tt, batch = lay.R, lay.TT, lay.B
    return pl.pallas_call(
        functools.partial(_final_norm_kernel, batch=batch),
        grid=(n // r,),
        in_specs=[pl.BlockSpec((r, d), lambda i: (i, 0)), _resident((1, d), lambda i: (0, 0))],
        out_specs=pl.BlockSpec((batch, tt, d), lambda i: (0, i, 0)),
        out_shape=jax.ShapeDtypeStruct((batch, n // batch, d), F32),
        scratch_shapes=[pltpu.VMEM((d // V7X_LANES, r, V7X_LANES), F32)],
        compiler_params=_params(("arbitrary",)),
        name="final_norm",
    )(x, g)


def _grid_pos_embed(rows, dim):
    row = np.repeat(np.arange(rows, dtype=np.float32), GRID_W)
    col = np.tile(np.arange(GRID_W, dtype=np.float32), rows)
    quarter = dim // 4
    freqs = jnp.float32(POS_THETA) ** (-jnp.arange(quarter, dtype=F32) / quarter)
    ar = jnp.asarray(row)[:, None] * freqs
    ac = jnp.asarray(col)[:, None] * freqs
    return jnp.concatenate([jnp.sin(ar), jnp.cos(ar), jnp.sin(ac), jnp.cos(ac)], axis=-1).astype(F32)


def kernel(x, c, ctx, c_ctx, norm1_g, norm2_g, final_norm_g, w_mod, b_mod, rg_w_in, rg_conv_w, rg_conv_b, rg_w_a, rg_b_a, rg_w_i, rg_b_i, rg_lam, rg_w_out, fn_w_out, router_w, router_b, exp_w_gate, exp_w_up, exp_w_down, sh_w_gate, sh_w_up, sh_w_down):
    batch, t_lat, d = x.shape
    t_ctx = ctx.shape[1]
    depth = w_mod.shape[0]
    lay = _Layout(batch, t_ctx, t_lat, with_ctx=True)
    lay_lat = _Layout(batch, t_ctx, t_lat, with_ctx=False)

    n_cond = 1 + batch
    pad = (-n_cond) % V7X_SUBLANES
    cond = jnp.concatenate([c_ctx[None], c, jnp.zeros((pad, d), F32)], axis=0)
    mods_all = _ada(cond, w_mod, b_mod)

    pos = _grid_pos_embed(t_lat // GRID_W, d)
    xall = _addpos(lay, ctx, x, pos)

    gd = d // FN_GROUPS
    cg, sg = _dft_tables(gd)
    cs_tab = jnp.asarray(np.concatenate([cg, sg], axis=1)).astype(BF16)
    ct_lat, st_lat = (jnp.asarray(a).astype(BF16) for a in _dft_tables(t_lat))
    ct_ctx, st_ctx = (jnp.asarray(a).astype(BF16) for a in _dft_tables(t_ctx))

    for i in range(depth):
        last = i == depth - 1
        j = i // N_MIXERS
        table = mods_all[i]
        mods = jnp.stack([jnp.broadcast_to(table[0:1], (lay.R, N_MOD * d)),
                          jnp.tile(table[1:1 + batch], (lay.TT, 1))])
        g1n = norm1_g[i][None]
        g2n = norm2_g[i][None]
        if i % N_MIXERS == 0:
            cur = lay
            gate, rec = _rg1(cur, xall, g1n, mods, rg_w_in[j].astype(BF16))
            hs = []
            for dr in range(2):
                hs.append(_rg2(cur, rec, rg_conv_w[j], rg_conv_b[j][None],
                               rg_w_a[j, dr].astype(BF16), rg_w_i[j, dr].astype(BF16),
                               rg_b_a[j, dr][None], rg_b_i[j, dr][None], rg_lam[j, dr][None],
                               rev=dr == 1))
            xall = _proj_res(cur, [hs[0], hs[1], gate], xall, mods, rg_w_out[j].astype(BF16))
        else:
            cur = lay_lat if last else lay
            x_off = lay.nct if last else 0
            zc, zs = _f1(lay, xall, g1n, mods, cs_tab, lay.nct, lay.nlt, latent=True)
            fre = [_f2(zc, zs, ct_lat, st_lat)]
            if not last:
                zc, zs = _f1(lay, xall, g1n, mods, cs_tab, 0, lay.nct, latent=False)
                fre = [_f2(zc, zs, ct_ctx, st_ctx)] + fre
            xall = _proj_res(cur, fre, xall, mods, fn_w_out[j].astype(BF16), x_off=x_off)
        if last and cur is lay:
            xall = xall[lay.nct * lay.R:]
            cur = lay_lat
        xall = _moe(cur, xall, g2n, mods, router_w[i].T, router_b[i][:, None],
                    exp_w_gate, exp_w_up, exp_w_down, i, sh_w_gate[i], sh_w_up[i], sh_w_down[i])

    return _final_norm(lay_lat, xall, final_norm_g[None])
```

```python
import functools
import math

import numpy as np
import jax
import jax.numpy as jnp
from jax import lax
from jax.experimental import pallas as pl
from jax.experimental.pallas import tpu as pltpu

GRID_W = 64
N_MIXERS = 2
LRU_HEADS = 8
CONV_W = 4
CONV_LEFT = 2
RG_C = 8.0
FN_GROUPS = 8
TOP_K = 8
N_GROUPS = 8
TOPK_GROUPS = 4
ROUTED_SCALE = 2.5
N_MOD = 6
EPS = 1e-6
POS_THETA = 10000.0

V7X_VMEM_BYTES = 64 * 1024 * 1024
V7X_SUBLANES = 8
V7X_LANES = 128
V7X_BF16_ROWS = 16
V7X_MXU_DIM = 256
MAX_ROW_TILE = 256
EXPERT_TILE_ROWS = 1024
EXPERT_SUBTILES = 2
VMEM_LIMIT = 56 * 1024 * 1024

F32 = jnp.float32
BF16 = jnp.bfloat16


def _params(sem, vmem=VMEM_LIMIT):
    return pltpu.CompilerParams(dimension_semantics=sem, vmem_limit_bytes=vmem)


def _resident(shape, index_map):
    return pl.BlockSpec(shape, index_map, pipeline_mode=pl.Buffered(1))


def _col_chunk(n, cap=512):
    c = min(n, cap)
    while n % c:
        c //= 2
    return c


def _normmod(x, g, shift, scale):
    ms = jnp.mean(x * x, axis=-1, keepdims=True)
    y = x * lax.rsqrt(ms + EPS) * g
    return y * (1.0 + scale) + shift


def _nt_dot(a, b):
    return lax.dot_general(a, b, (((1,), (1,)), ((), ())), preferred_element_type=F32)


def _split_by_batch(tile, s3, batch):
    r, d = tile.shape
    tt = r // batch
    nch = d // V7X_LANES
    for c in range(nch):
        s3[c] = tile[:, c * V7X_LANES:(c + 1) * V7X_LANES]
    return [jnp.concatenate([s3[c, pl.ds(b, tt, stride=batch), :] for c in range(nch)], axis=1)
            for b in range(batch)]


def _merge_by_batch(pieces, s3):
    batch = len(pieces)
    tt, d = pieces[0].shape
    nch = d // V7X_LANES
    for b, piece in enumerate(pieces):
        for c in range(nch):
            s3[c, pl.ds(b, tt, stride=batch), :] = piece[:, c * V7X_LANES:(c + 1) * V7X_LANES]
    return jnp.concatenate([s3[c] for c in range(nch)], axis=1)


class _Layout:
    def __init__(self, batch, t_ctx, t_lat, with_ctx=True):
        assert batch % V7X_SUBLANES == 0
        g = math.gcd(t_ctx, t_lat)
        fits = [tt for tt in range(1, g + 1)
                if g % tt == 0 and tt * batch <= MAX_ROW_TILE and (tt * batch) % V7X_BF16_ROWS == 0]
        tt = max(fits)
        self.TT = tt
        self.R = tt * batch
        self.B = batch
        self.Tc = t_ctx
        self.T = t_lat
        self.nct = t_ctx // tt if with_ctx else 0
        self.nlt = t_lat // tt
        self.ntiles = self.nct + self.nlt
        self.N = self.ntiles * self.R


def _mod_spec(lay, d, which):
    return pl.BlockSpec((None, lay.R, d), lambda i, *_: (jnp.where(i < lay.nct, 0, 1), 0, which))


def _ada_kernel(cond_ref, w_ref, b_ref, out_ref):
    s = jax.nn.silu(cond_ref[...]).astype(BF16)
    out_ref[...] = jnp.dot(s, w_ref[...].astype(BF16), preferred_element_type=F32) + b_ref[...]


def _ada(cond, w_mod, b_mod):
    depth, d, nm = w_mod.shape
    rows = cond.shape[0]
    tn = _col_chunk(nm, 1024)
    return pl.pallas_call(
        _ada_kernel,
        grid=(depth, nm // tn),
        in_specs=[
            pl.BlockSpec((rows, d), lambda l, j: (0, 0)),
            pl.BlockSpec((None, d, tn), lambda l, j: (l, 0, j)),
            pl.BlockSpec((None, 1, tn), lambda l, j: (l, 0, j)),
        ],
        out_specs=pl.BlockSpec((None, rows, tn), lambda l, j: (l, 0, j)),
        out_shape=jax.ShapeDtypeStruct((depth, rows, nm), F32),
        compiler_params=_params(("arbitrary", "arbitrary")),
        name="ada_mod",
    )(cond, w_mod, b_mod.reshape(depth, 1, nm))


def _addpos_kernel(ctx_ref, x_ref, pos_ref, out_ref, s3, *, nct, batch):
    i = pl.program_id(0)

    @pl.when(i < nct)
    def _():
        out_ref[...] = _merge_by_batch([ctx_ref[b] for b in range(batch)], s3)

    @pl.when(i >= nct)
    def _():
        out_ref[...] = _merge_by_batch([x_ref[b] + pos_ref[...] for b in range(batch)], s3)


def _addpos(lay, ctx, x, pos):
    batch, _, d = x.shape
    r, tt = lay.R, lay.TT
    return pl.pallas_call(
        functools.partial(_addpos_kernel, nct=lay.nct, batch=batch),
        grid=(lay.ntiles,),
        in_specs=[
            pl.BlockSpec((batch, tt, d), lambda i: (0, jnp.minimum(i, lay.nct - 1), 0)),
            pl.BlockSpec((batch, tt, d), lambda i: (0, jnp.maximum(i - lay.nct, 0), 0)),
            pl.BlockSpec((tt, d), lambda i: (jnp.maximum(i - lay.nct, 0), 0)),
        ],
        out_specs=pl.BlockSpec((r, d), lambda i: (i, 0)),
        out_shape=jax.ShapeDtypeStruct((lay.N, d), F32),
        scratch_shapes=[pltpu.VMEM((d // V7X_LANES, r, V7X_LANES), F32)],
        compiler_params=_params(("arbitrary",)),
        name="add_pos",
    )(ctx, x, pos)


def _rg1_kernel(x_ref, g_ref, sh_ref, sc_ref, w_ref, gate_ref, rec_ref, *, width, cn):
    h = _normmod(x_ref[...], g_ref[...], sh_ref[...], sc_ref[...]).astype(BF16)
    for c in range(2 * width // cn):
        z = jnp.dot(h, w_ref[:, c * cn:(c + 1) * cn], preferred_element_type=F32)
        if c * cn < width:
            gate_ref[:, c * cn:(c + 1) * cn] = jax.nn.gelu(z).astype(BF16)
        else:
            rec_ref[:, c * cn - width:(c + 1) * cn - width] = z.astype(BF16)


def _rg1(lay, x, g, mods, w_in):
    d = x.shape[1]
    width = w_in.shape[1] // 2
    r = lay.R
    cn = _col_chunk(width)
    row = lambda i: (i, 0)
    return pl.pallas_call(
        functools.partial(_rg1_kernel, width=width, cn=cn),
        grid=(lay.ntiles,),
        in_specs=[
            pl.BlockSpec((r, d), row),
            _resident((1, d), lambda i: (0, 0)),
            _mod_spec(lay, d, 0),
            _mod_spec(lay, d, 1),
            _resident((d, 2 * width), lambda i: (0, 0)),
        ],
        out_specs=[pl.BlockSpec((r, width), row), pl.BlockSpec((r, width), row)],
        out_shape=[jax.ShapeDtypeStruct((lay.N, width), BF16)] * 2,
        compiler_params=_params(("arbitrary",)),
        name="rg_in_proj",
    )(x, g, mods, mods, w_in)


def _log_sigmoid(x):
    return jnp.minimum(x, 0.0) - jnp.log1p(jnp.exp(-jnp.abs(x)))


def _neg_expm1(t, exp_t):
    series = t * (1.0 + t * (0.5 + t * (1.0 / 6.0 + t * (1.0 / 24.0 + t * (1.0 / 120.0)))))
    return -jnp.where(t > -0.1, series, exp_t - 1.0)


def _rg2_kernel(cur_ref, prev_ref, next_ref, cw_ref, cb_ref, wa_ref, wi_ref, ba_ref, bi_ref,
                lam_ref, out_ref, carry_ref, *, rev, nct, nlt, tt, batch, hd, heads):
    s = pl.program_id(0)

    @pl.when(s == 0)
    def _():
        carry_ref[...] = jnp.zeros_like(carry_ref)

    in_ctx = s < nct
    pos = jnp.where(in_ctx, s, s - nct)
    n_seq = jnp.where(in_ctx, nct, nlt)
    chunk = (n_seq - 1 - pos) if rev else pos
    keep_prev = jnp.where(chunk == 0, 0.0, 1.0).astype(F32)
    keep_next = jnp.where(chunk == n_seq - 1, 0.0, 1.0).astype(F32)
    r = tt * batch

    for h in range(heads):
        sl = slice(h * hd, (h + 1) * hd)
        cur = cur_ref[:, sl].astype(F32)
        pv = prev_ref[:, sl].astype(F32) * keep_prev
        nx = next_ref[:, sl].astype(F32) * keep_next
        ext = jnp.concatenate([pv, cur, nx], axis=0)
        u = cb_ref[:, sl] + cw_ref[0:1, sl] * ext[0:r]
        for k in range(1, CONV_W):
            u = u + cw_ref[k:k + 1, sl] * ext[k * batch:k * batch + r]
        ub = u.astype(BF16)
        ra = jnp.dot(ub, wa_ref[h], preferred_element_type=F32) + ba_ref[:, sl]
        ri = jnp.dot(ub, wi_ref[h], preferred_element_type=F32) + bi_ref[:, sl]
        log_a = RG_C * jax.nn.sigmoid(ra) * _log_sigmoid(lam_ref[:, sl])
        a = jnp.exp(log_a)
        b = jnp.sqrt(_neg_expm1(2.0 * log_a, a * a)) * (jax.nn.sigmoid(ri) * u)

        state = carry_ref[:, sl]
        outs = [None] * tt
        for step in (range(tt - 1, -1, -1) if rev else range(tt)):
            rows = slice(step * batch, (step + 1) * batch)
            state = a[rows] * state + b[rows]
            outs[step] = state
        carry_ref[:, sl] = state
        out_ref[:, sl] = jnp.concatenate(outs, axis=0).astype(BF16)


def _rg2(lay, rec, conv_w, conv_b, w_a, w_i, b_a, b_i, lam, rev):
    n, width = rec.shape
    r, nct, nlt, batch = lay.R, lay.nct, lay.nlt, lay.B
    heads = w_a.shape[0]
    hd = width // heads
    left = CONV_LEFT * batch
    right = (CONV_W - 1 - CONV_LEFT) * batch
    assert r % left == 0 and r % right == 0

    def tile(s):
        if rev:
            return jnp.where(s < nct, nct - 1 - s, nct + nlt - 1 - (s - nct))
        return s

    const2 = lambda s: (0, 0)
    const3 = lambda s: (0, 0, 0)
    return pl.pallas_call(
        functools.partial(_rg2_kernel, rev=rev, nct=nct, nlt=nlt, tt=lay.TT, batch=batch, hd=hd, heads=heads),
        grid=(nct + nlt,),
        in_specs=[
            pl.BlockSpec((r, width), lambda s: (tile(s), 0)),
            pl.BlockSpec((left, width), lambda s: (jnp.maximum(tile(s) * (r // left) - 1, 0), 0)),
            pl.BlockSpec((right, width), lambda s: (jnp.minimum((tile(s) + 1) * (r // right), n // right - 1), 0)),
            _resident((CONV_W, width), const2),
            _resident((1, width), const2),
            _resident((heads, hd, hd), const3),
            _resident((heads, hd, hd), const3),
            _resident((1, width), const2),
            _resident((1, width), const2),
            _resident((1, width), const2),
        ],
        out_specs=pl.BlockSpec((r, width), lambda s: (tile(s), 0)),
        out_shape=jax.ShapeDtypeStruct((n, width), BF16),
        scratch_shapes=[pltpu.VMEM((batch, width), F32)],
        compiler_params=_params(("arbitrary",)),
        name="rg_scan_bwd" if rev else "rg_scan_fwd",
    )(rec, rec, rec, conv_w, conv_b, w_a, w_i, b_a, b_i, lam)


def _proj_res_kernel(*refs, mode, nct, cn, batch):
    if mode == "gated":
        hf_ref, hb_ref, gate_ref, x_ref, g1_ref, w_ref, out_ref = refs
        a = ((hf_ref[...].astype(F32) + hb_ref[...].astype(F32)) * gate_ref[...].astype(F32)).astype(BF16)
    else:
        if mode == "ctx_lat":
            ac_ref, al_ref, x_ref, g1_ref, w_ref, out_ref, a_s = refs
            blk = jnp.where(pl.program_id(0) < nct, ac_ref[...], al_ref[...])
        else:
            a_ref, x_ref, g1_ref, w_ref, out_ref, a_s = refs
            blk = a_ref[...]
        k = blk.shape[1] // batch
        a = _merge_by_batch([blk[:, b * k:(b + 1) * k].astype(F32) for b in range(batch)], a_s).astype(BF16)
    d = out_ref.shape[1]
    for c in range(d // cn):
        cs = slice(c * cn, (c + 1) * cn)
        y = jnp.dot(a, w_ref[:, cs], preferred_element_type=F32)
        out_ref[:, cs] = x_ref[:, cs] + g1_ref[:, cs] * y


def _proj_res(lay, a_list, x, mods, w_out, x_off=0):
    k, d = w_out.shape
    r = lay.R
    cn = _col_chunk(d)
    row = lambda i: (i, 0)
    mode = {1: "plain", 2: "ctx_lat", 3: "gated"}[len(a_list)]
    if mode == "ctx_lat":
        a_maps = [lambda i: (jnp.minimum(i, lay.nct - 1), 0), lambda i: (jnp.maximum(i - lay.nct, 0), 0)]
    else:
        a_maps = [row] * len(a_list)
    a_block = (r, k) if mode == "gated" else (lay.TT, lay.B * k)
    scratch = [] if mode == "gated" else [pltpu.VMEM((k // V7X_LANES, r, V7X_LANES), F32)]
    return pl.pallas_call(
        functools.partial(_proj_res_kernel, mode=mode, nct=lay.nct, cn=cn, batch=lay.B),
        grid=(lay.ntiles,),
        in_specs=[pl.BlockSpec(a_block, m) for m in a_maps] + [
            pl.BlockSpec((r, d), lambda i: (i + x_off, 0)),
            _mod_spec(lay, d, 2),
            _resident((k, d), lambda i: (0, 0)),
        ],
        out_specs=pl.BlockSpec((r, d), row),
        out_shape=jax.ShapeDtypeStruct((lay.N, d), F32),
        scratch_shapes=scratch,
        compiler_params=_params(("arbitrary",)),
        name="mixer_out_proj",
    )(*a_list, x, mods, w_out)


def _dft_tables(n):
    k = np.arange(n, dtype=np.int64)
    ang = 2.0 * np.pi * ((k[:, None] * k[None, :]) % n).astype(np.float64) / n
    s = 1.0 / np.sqrt(n)
    return (np.cos(ang) * s).astype(np.float32), (np.sin(ang) * s).astype(np.float32)


def _f1_kernel(x_ref, g_ref, sh_ref, sc_ref, cs_ref, zc_ref, zs_ref, s3, *, groups, gd, batch):
    h = _normmod(x_ref[...], g_ref[...], sh_ref[...], sc_ref[...]).astype(BF16)
    zs = [jnp.dot(h[:, gi * gd:(gi + 1) * gd], cs_ref[...], preferred_element_type=F32) for gi in range(groups)]
    d = groups * gd
    for out_ref, part in ((zc_ref, slice(0, gd)), (zs_ref, slice(gd, 2 * gd))):
        tile = jnp.concatenate([z[:, part] for z in zs], axis=1)
        for b, piece in enumerate(_split_by_batch(tile, s3, batch)):
            out_ref[:, b * d:(b + 1) * d] = piece.astype(BF16)


def _f1(lay, x, g, mods, cs, first_tile, n_tiles, latent):
    d = x.shape[1]
    r, tt, batch = lay.R, lay.TT, lay.B
    gd = d // FN_GROUPS
    row = lambda i: (i, 0)
    mod = lambda which: pl.BlockSpec((None, r, d), lambda i: (int(latent), 0, which))
    return pl.pallas_call(
        functools.partial(_f1_kernel, groups=FN_GROUPS, gd=gd, batch=batch),
        grid=(n_tiles,),
        in_specs=[
            pl.BlockSpec((r, d), lambda i: (i + first_tile, 0)),
            _resident((1, d), lambda i: (0, 0)),
            mod(0),
            mod(1),
            _resident((gd, 2 * gd), lambda i: (0, 0)),
        ],
        out_specs=[pl.BlockSpec((tt, batch * d), row), pl.BlockSpec((tt, batch * d), row)],
        out_shape=[jax.ShapeDtypeStruct((n_tiles * tt, batch * d), BF16)] * 2,
        scratch_shapes=[pltpu.VMEM((d // V7X_LANES, r, V7X_LANES), F32)],
        compiler_params=_params(("arbitrary",)),
        name="fourier_channel_dft",
    )(x, g, mods, mods, cs)


def _f2_kernel(ct_ref, st_ref, zc_ref, zs_ref, out_ref, *, t, rc):
    for c in range(t // rc):
        rs = slice(c * rc, (c + 1) * rc)
        y = jnp.dot(ct_ref[rs, :], zc_ref[...], preferred_element_type=F32)
        y = y - jnp.dot(st_ref[rs, :], zs_ref[...], preferred_element_type=F32)
        out_ref[rs, :] = y.astype(BF16)


def _f2(zc, zs, ct, st):
    t, cols = zc.shape
    tn = _col_chunk(cols)
    rc = _col_chunk(t)
    zmap = lambda j: (0, j)
    return pl.pallas_call(
        functools.partial(_f2_kernel, t=t, rc=rc),
        grid=(cols // tn,),
        in_specs=[
            _resident((t, t), lambda j: (0, 0)),
            _resident((t, t), lambda j: (0, 0)),
            pl.BlockSpec((t, tn), zmap),
            pl.BlockSpec((t, tn), zmap),
        ],
        out_specs=pl.BlockSpec((t, tn), zmap),
        out_shape=jax.ShapeDtypeStruct((t, cols), BF16),
        compiler_params=_params(("arbitrary",)),
        name="fourier_time_dft",
    )(ct, st, zc, zs)


def _pack_pairs(v):
    half = v.shape[1] // 2
    bits = pltpu.bitcast(v.astype(BF16).astype(F32), jnp.uint32)
    return (bits[:, :half] >> 16) | (bits[:, half:] & jnp.uint32(0xFFFF0000))


def _unpack_pairs(w):
    lo = pltpu.bitcast(w << 16, F32)
    hi = pltpu.bitcast(w & jnp.uint32(0xFFFF0000), F32)
    return lo, hi


def _router_kernel(x_ref, g_ref, sh_ref, sc_ref, rw_ref, rb_ref, tri_e_ref, tri_t_ref,
                   f_ref, ek_ref, rk_ref, wk_ref, cnt_ref, base_ref, *, n_exp):
    @pl.when(pl.program_id(0) == 0)
    def _():
        base_ref[...] = jnp.zeros_like(base_ref)

    f = _normmod(x_ref[...], g_ref[...], sh_ref[...], sc_ref[...])
    f_hi = f.astype(BF16)
    f_lo = (f - f_hi.astype(F32)).astype(BF16)
    f_ref[...] = _pack_pairs(f)
    rw = rw_ref[...]
    rw_hi = rw.astype(BF16)
    rw_lo = (rw - rw_hi.astype(F32)).astype(BF16)
    logits = _nt_dot(rw_hi, f_hi) + (_nt_dot(rw_hi, f_lo) + _nt_dot(rw_lo, f_hi))
    scores = jax.nn.sigmoid(logits)
    biased = scores + rb_ref[...]
    r = scores.shape[1]
    gsz = n_exp // N_GROUPS
    sub = lax.broadcasted_iota(jnp.int32, (gsz, r), 0)
    neg = jnp.float32(-jnp.inf)

    blocks = [biased[gi * gsz:(gi + 1) * gsz] for gi in range(N_GROUPS)]
    gscore = []
    for blk in blocks:
        m1 = jnp.max(blk, axis=0, keepdims=True)
        first = jnp.min(jnp.where(blk == m1, sub, gsz), axis=0, keepdims=True)
        m2 = jnp.max(jnp.where(sub == first, neg, blk), axis=0, keepdims=True)
        gscore.append(m1 + m2)
    masked = []
    for gi in range(N_GROUPS):
        rank = jnp.zeros((1, r), jnp.int32)
        for gj in range(N_GROUPS):
            if gj == gi:
                continue
            ahead = (gscore[gj] >= gscore[gi]) if gj < gi else (gscore[gj] > gscore[gi])
            rank = rank + ahead.astype(jnp.int32)
        masked.append(jnp.where(rank < TOPK_GROUPS, blocks[gi], neg))
    e_idx = [sub + gi * gsz for gi in range(N_GROUPS)]
    live = list(masked)
    picked = [jnp.zeros((gsz, r), jnp.bool_) for _ in range(N_GROUPS)]
    for _ in range(TOP_K):
        best = live[0]
        for gi in range(1, N_GROUPS):
            best = jnp.maximum(best, live[gi])
        best = jnp.max(best, axis=0, keepdims=True)
        first = jnp.where(live[0] == best, e_idx[0], n_exp)
        for gi in range(1, N_GROUPS):
            first = jnp.minimum(first, jnp.where(live[gi] == best, e_idx[gi], n_exp))
        first = jnp.min(first, axis=0, keepdims=True)
        for gi in range(N_GROUPS):
            hit = e_idx[gi] == first
            picked[gi] = picked[gi] | hit
            live[gi] = jnp.where(hit, neg, live[gi])
    sel_f = [jnp.where(pk, 1.0, 0.0) for pk in picked]
    sel_w = [jnp.where(pk, scores[gi * gsz:(gi + 1) * gsz], 0.0) for gi, pk in enumerate(picked)]
    tot = sel_w[0]
    for gi in range(1, N_GROUPS):
        tot = tot + sel_w[gi]
    denom = jnp.sum(tot, axis=0, keepdims=True)
    gates = jnp.concatenate(sel_w, axis=0) / denom * ROUTED_SCALE
    sel = jnp.concatenate(sel_f, axis=0)
    sel_b = sel.astype(BF16)
    slot = jnp.dot(tri_e_ref[...], sel_b, preferred_element_type=F32)
    base = base_ref[...]
    pos_in_expert = base + jnp.dot(sel_b, tri_t_ref[...], preferred_element_type=F32)
    base = base + jnp.sum(sel, axis=1, keepdims=True)
    base_ref[...] = base
    cnt_ref[...] = base
    e_idx = lax.broadcasted_iota(jnp.int32, sel.shape, 0).astype(F32)
    for k in range(TOP_K):
        hit = (sel > 0.0) & (slot == float(k))
        ek_ref[k:k + 1, :] = jnp.sum(jnp.where(hit, e_idx, 0.0), axis=0, keepdims=True).astype(jnp.int32)
        rk_ref[k:k + 1, :] = jnp.sum(jnp.where(hit, pos_in_expert, 0.0), axis=0, keepdims=True).astype(jnp.int32)
        wk_ref[k:k + 1, :] = jnp.sum(jnp.where(hit, gates, 0.0), axis=0, keepdims=True)


def _router(lay, x, g, mods, rw_t, rb):
    d = x.shape[1]
    n_exp = rw_t.shape[0]
    r = lay.R
    row = lambda i: (i, 0)
    tok = lambda i: (0, i)
    const = lambda i: (0, 0)
    tri_e = jnp.asarray(np.tril(np.ones((n_exp, n_exp), np.float32), -1)).astype(BF16)
    tri_t = jnp.asarray(np.triu(np.ones((r, r), np.float32), 1)).astype(BF16)
    return pl.pallas_call(
        functools.partial(_router_kernel, n_exp=n_exp),
        grid=(lay.ntiles,),
        in_specs=[
            pl.BlockSpec((r, d), row),
            _resident((1, d), const),
            _mod_spec(lay, d, 3),
            _mod_spec(lay, d, 4),
            _resident((n_exp, d), const),
            _resident((n_exp, 1), const),
            _resident((n_exp, n_exp), const),
            _resident((r, r), const),
        ],
        out_specs=[
            pl.BlockSpec((r, d // 2), row),
            pl.BlockSpec((TOP_K, r), tok),
            pl.BlockSpec((TOP_K, r), tok),
            pl.BlockSpec((TOP_K, r), tok),
            pl.BlockSpec((n_exp, 1), const),
        ],
        out_shape=[
            jax.ShapeDtypeStruct((lay.N, d // 2), jnp.uint32),
            jax.ShapeDtypeStruct((TOP_K, lay.N), jnp.int32),
            jax.ShapeDtypeStruct((TOP_K, lay.N), jnp.int32),
            jax.ShapeDtypeStruct((TOP_K, lay.N), F32),
            jax.ShapeDtypeStruct((n_exp, 1), F32),
        ],
        scratch_shapes=[pltpu.VMEM((n_exp, 1), F32)],
        compiler_params=_params(("arbitrary",)),
        name="moe_router",
    )(x, g, mods, mods, rw_t, rb, tri_e, tri_t)


def _dispatch_kernel(pad_start_ref, pad_len_ref, nused_ref, p_ref, f_ref, xs_ref, sem, zeros_ref,
                     *, n_exp, r, tmx):
    i = pl.program_id(0)

    def scatter_rows(t, carry):
        for k in range(TOP_K):
            p = p_ref[t * TOP_K + k]
            pltpu.make_async_copy(f_ref.at[pl.ds(t, 1)], xs_ref.at[pl.ds(p, 1)], sem.at[0]).start()
        return carry

    lax.fori_loop(0, r, scatter_rows, 0)
    pltpu.make_async_copy(xs_ref.at[pl.ds(0, r * TOP_K)], xs_ref.at[pl.ds(0, r * TOP_K)], sem.at[0]).wait()

    @pl.when(i == pl.num_programs(0) - 1)
    def _():
        zeros_ref[...] = jnp.zeros_like(zeros_ref)

        def pad_copies(e, act):
            start = pad_start_ref[e]
            length = pad_len_ref[e]
            head = jnp.minimum(length, (-start) & (V7X_SUBLANES - 1))
            base = start + head

            def row(j, c):
                act(pltpu.make_async_copy(zeros_ref.at[pl.ds(0, 1)], xs_ref.at[pl.ds(start + j, 1)], sem.at[1]))
                return c

            def group(j, c):
                off = pl.multiple_of(base + j * V7X_SUBLANES, V7X_SUBLANES)
                act(pltpu.make_async_copy(zeros_ref.at[pl.ds(0, V7X_SUBLANES)],
                                          xs_ref.at[pl.ds(off, V7X_SUBLANES)], sem.at[2]))
                return c

            lax.fori_loop(0, head, row, 0)
            lax.fori_loop(0, (length - head) // V7X_SUBLANES, group, 0)

        def tile_copy(m, act):
            dst = xs_ref.at[pl.ds(pl.multiple_of(m * tmx, tmx), tmx)]
            act(pltpu.make_async_copy(zeros_ref, dst, sem.at[3]))

        n_tiles = xs_ref.shape[0] // tmx
        for act in (lambda cp: cp.start(), lambda cp: cp.wait()):
            lax.fori_loop(0, n_exp, lambda e, c, act=act: (pad_copies(e, act), c)[1], 0)
            lax.fori_loop(nused_ref[0], n_tiles, lambda m, c, act=act: (tile_copy(m, act), c)[1], 0)


def _dispatch(lay, fpk, p_flat, pad_start, pad_len, n_used, n_rows, tmx):
    n, half = fpk.shape
    r = lay.R
    n_exp = pad_start.shape[0]
    grid_spec = pltpu.PrefetchScalarGridSpec(
        num_scalar_prefetch=3,
        grid=(lay.ntiles,),
        in_specs=[
            pl.BlockSpec((r * TOP_K,), lambda i, *_: (i,), memory_space=pltpu.SMEM),
            pl.BlockSpec((r, half), lambda i, *_: (i, 0)),
        ],
        out_specs=pl.BlockSpec(memory_space=pl.ANY),
        scratch_shapes=[pltpu.SemaphoreType.DMA((4,)), pltpu.VMEM((tmx, half), jnp.uint32)],
    )
    return pl.pallas_call(
        functools.partial(_dispatch_kernel, n_exp=n_exp, r=r, tmx=tmx),
        grid_spec=grid_spec,
        out_shape=jax.ShapeDtypeStruct((n_rows, half), jnp.uint32),
        compiler_params=_params(("arbitrary",)),
        name="moe_dispatch",
    )(pad_start, pad_len, n_used, p_flat, fpk)


def _swiglu_packed(xw, wg_ref, wu_ref, wd_ref):
    half = xw.shape[1]
    lo, hi = _unpack_pairs(xw)
    lo = lo.astype(BF16)
    hi = hi.astype(BF16)

    def proj(w_ref):
        top = jnp.dot(lo, w_ref[:half, :], preferred_element_type=F32)
        return top + jnp.dot(hi, w_ref[half:, :], preferred_element_type=F32)

    hid = (jax.nn.silu(proj(wg_ref)) * proj(wu_ref)).astype(BF16)
    return jnp.dot(hid, wd_ref[...], preferred_element_type=F32)


def _grouped_kernel(te_ref, nused_ref, xs_ref, wg_ref, wu_ref, wd_ref, ys_ref, wg_b, wu_b, wd_b):
    m = pl.program_id(0)

    @pl.when((m == 0) | (te_ref[m] != te_ref[jnp.maximum(m - 1, 0)]))
    def _():
        wg_b[...] = wg_ref[...].astype(BF16)
        wu_b[...] = wu_ref[...].astype(BF16)
        wd_b[...] = wd_ref[...].astype(BF16)

    @pl.when(m < nused_ref[0])
    def _():
        rows = xs_ref.shape[0] // EXPERT_SUBTILES
        for c in range(EXPERT_SUBTILES):
            rs = slice(c * rows, (c + 1) * rows)
            ys_ref[rs, :] = _pack_pairs(_swiglu_packed(xs_ref[rs, :], wg_b, wu_b, wd_b))

    @pl.when(m >= nused_ref[0])
    def _():
        ys_ref[...] = jnp.zeros_like(ys_ref)


def _grouped(xs, wg, wu, wd, layer, tile_expert, n_used, tmx):
    n_rows, half = xs.shape
    _, _, d, fdim = wg.shape
    grid_spec = pltpu.PrefetchScalarGridSpec(
        num_scalar_prefetch=2,
        grid=(n_rows // tmx,),
        in_specs=[
            pl.BlockSpec((tmx, half), lambda m, te, nu: (jnp.minimum(m, nu[0] - 1), 0)),
            pl.BlockSpec((None, None, d, fdim), lambda m, te, nu: (layer, te[m], 0, 0)),
            pl.BlockSpec((None, None, d, fdim), lambda m, te, nu: (layer, te[m], 0, 0)),
            pl.BlockSpec((None, None, fdim, d), lambda m, te, nu: (layer, te[m], 0, 0)),
        ],
        out_specs=pl.BlockSpec((tmx, half), lambda m, te, nu: (m, 0)),
        scratch_shapes=[pltpu.VMEM((d, fdim), BF16), pltpu.VMEM((d, fdim), BF16), pltpu.VMEM((fdim, d), BF16)],
    )
    return pl.pallas_call(
        _grouped_kernel,
        grid_spec=grid_spec,
        out_shape=jax.ShapeDtypeStruct((n_rows, half), jnp.uint32),
        compiler_params=_params(("arbitrary",)),
        name="moe_experts",
    )(tile_expert, n_used, xs, wg, wu, wd)


def _combine_kernel(p0_ref, pn_ref, w_ref, f_ref, x_ref, g2_ref, swg_ref, swu_ref, swd_ref, ys_ref, out_ref,
                    buf_a, buf_b, sh_ref, sem, *, r):
    i = pl.program_id(0)
    last = pl.num_programs(0) - 1
    half = f_ref.shape[1]
    rows8 = V7X_SUBLANES

    def issue(p_ref, dst, dsem, t0, count):
        for j in range(count):
            t = t0 + j
            for k in range(TOP_K):
                p = p_ref[t * TOP_K + k]
                pltpu.make_async_copy(ys_ref.at[pl.ds(p, 1)], dst.at[k, pl.ds(t, 1)], dsem).start()

    @pl.when(i == 0)
    def _():
        lax.fori_loop(0, r, lambda t, c: (issue(p0_ref, buf_a, sem.at[0], t, 1), c)[1], 0)

    def run(cur, cur_sem, nxt, nxt_sem):
        pltpu.make_async_copy(cur, cur, cur_sem).wait()
        sh_ref[...] = _swiglu_packed(f_ref[...], swg_ref, swu_ref, swd_ref)

        def finish_rows(g):
            rows = pl.ds(pl.multiple_of(g * rows8, rows8), rows8)
            acc_lo = sh_ref[rows, :half]
            acc_hi = sh_ref[rows, half:]
            w = w_ref[rows, :]
            for k in range(TOP_K):
                lo, hi = _unpack_pairs(cur[k, rows, :])
                wcol = w[:, k:k + 1]
                acc_lo = acc_lo + wcol * lo
                acc_hi = acc_hi + wcol * hi
            out_ref[rows, :half] = x_ref[rows, :half] + g2_ref[rows, :half] * acc_lo
            out_ref[rows, half:] = x_ref[rows, half:] + g2_ref[rows, half:] * acc_hi

        @pl.when(i < last)
        def _():
            def body(g, c):
                issue(pn_ref, nxt, nxt_sem, pl.multiple_of(g * rows8, rows8), rows8)
                finish_rows(g)
                return c

            lax.fori_loop(0, r // rows8, body, 0)

        @pl.when(i == last)
        def _():
            lax.fori_loop(0, r // rows8, lambda g, c: (finish_rows(g), c)[1], 0)

    @pl.when(i % 2 == 0)
    def _():
        run(buf_a, sem.at[0], buf_b, sem.at[1])

    @pl.when(i % 2 == 1)
    def _():
        run(buf_b, sem.at[1], buf_a, sem.at[0])


def _combine(lay, ys, p_flat, w_tok, fpk, x, mods, swg, swu, swd):
    n, d = x.shape
    half = d // 2
    r = lay.R
    fdim = swd.shape[0]
    row = lambda i: (i, 0)
    const = lambda i: (0, 0)
    gather_buf = pltpu.VMEM((TOP_K, r, half), jnp.uint32)
    return pl.pallas_call(
        functools.partial(_combine_kernel, r=r),
        grid=(lay.ntiles,),
        in_specs=[
            pl.BlockSpec((r * TOP_K,), lambda i: (i,), memory_space=pltpu.SMEM),
            pl.BlockSpec((r * TOP_K,), lambda i: (jnp.minimum(i + 1, lay.ntiles - 1),), memory_space=pltpu.SMEM),
            pl.BlockSpec((r, TOP_K), row),
            pl.BlockSpec((r, half), row),
            pl.BlockSpec((r, d), row),
            _mod_spec(lay, d, 5),
            _resident((d, fdim), const),
            _resident((d, fdim), const),
            _resident((fdim, d), const),
            pl.BlockSpec(memory_space=pl.ANY),
        ],
        out_specs=pl.BlockSpec((r, d), row),
        out_shape=jax.ShapeDtypeStruct((n, d), F32),
        scratch_shapes=[gather_buf, gather_buf, pltpu.VMEM((r, d), F32), pltpu.SemaphoreType.DMA((2,))],
        compiler_params=_params(("arbitrary",)),
        name="moe_combine",
    )(p_flat, p_flat, w_tok, fpk, x, mods, swg, swu, swd, ys)


def _moe(lay, x, g2n, mods, rw_t, rb, wg, wu, wd, layer, swg, swu, swd):
    n_exp = rw_t.shape[0]
    tmx = EXPERT_TILE_ROWS // lay.R * lay.R if lay.R <= EXPERT_TILE_ROWS else lay.R
    fpk, ek, rk, wk, cnt = _router(lay, x, g2n, mods, rw_t, rb)
    cnt = cnt[:, 0].astype(jnp.int32)
    padded = (cnt + tmx - 1) // tmx * tmx
    offs = jnp.concatenate([jnp.zeros((1,), jnp.int32), jnp.cumsum(padded)])
    n_tiles = lay.N * TOP_K // tmx + n_exp
    e_ids = jnp.arange(n_exp, dtype=jnp.int32)[:, None, None]
    seg_start = jnp.sum(jnp.where(ek[None] > e_ids, padded[:, None, None], 0), axis=0)
    p_flat = (seg_start + rk).T.reshape(-1)
    w_tok = wk.T
    n_used = (offs[n_exp] // tmx).reshape(1)
    tile_start = jnp.minimum(jnp.arange(n_tiles, dtype=jnp.int32), n_used[0] - 1) * tmx
    tile_expert = jnp.sum((tile_start[:, None] >= offs[None, 1:]).astype(jnp.int32), axis=1)
    xs = _dispatch(lay, fpk, p_flat, offs[:-1] + cnt, padded - cnt, n_used, n_tiles * tmx, tmx)
    ys = _grouped(xs, wg, wu, wd, layer, tile_expert, n_used, tmx)
    return _combine(lay, ys, p_flat, w_tok, fpk, x, mods, swg.astype(BF16), swu.astype(BF16), swd.astype(BF16))


def _final_norm_kernel(x_ref, g_ref, out_ref, s3, *, batch):
    x = x_ref[...]
    ms = jnp.mean(x * x, axis=-1, keepdims=True)
    y = x * lax.rsqrt(ms + EPS) * g_ref[...]
    for b, piece in enumerate(_split_by_batch(y, s3, batch)):
        out_ref[b] = piece


def _final_norm(lay, x, g):
    n, d = x.shape
    r, tt, batch = lay.R, lay.TT, lay.B
    return pl.pallas_call(
        functools.partial(_final_norm_kernel, batch=batch),
        grid=(n // r,),
        in_specs=[pl.BlockSpec((r, d), lambda i: (i, 0)), _resident((1, d), lambda i: (0, 0))],
        out_specs=pl.BlockSpec((batch, tt, d), lambda i: (0, i, 0)),
        out_shape=jax.ShapeDtypeStruct((batch, n // batch, d), F32),
        scratch_shapes=[pltpu.VMEM((d // V7X_LANES, r, V7X_LANES), F32)],
        compiler_params=_params(("arbitrary",)),
        name="final_norm",
    )(x, g)


def _grid_pos_embed(rows, dim):
    row = np.repeat(np.arange(rows, dtype=np.float32), GRID_W)
    col = np.tile(np.arange(GRID_W, dtype=np.float32), rows)
    quarter = dim // 4
    freqs = jnp.float32(POS_THETA) ** (-jnp.arange(quarter, dtype=F32) / quarter)
    ar = jnp.asarray(row)[:, None] * freqs
    ac = jnp.asarray(col)[:, None] * freqs
    return jnp.concatenate([jnp.sin(ar), jnp.cos(ar), jnp.sin(ac), jnp.cos(ac)], axis=-1).astype(F32)


def kernel(x, c, ctx, c_ctx, norm1_g, norm2_g, final_norm_g, w_mod, b_mod, rg_w_in, rg_conv_w, rg_conv_b, rg_w_a, rg_b_a, rg_w_i, rg_b_i, rg_lam, rg_w_out, fn_w_out, router_w, router_b, exp_w_gate, exp_w_up, exp_w_down, sh_w_gate, sh_w_up, sh_w_down):
    batch, t_lat, d = x.shape
    t_ctx = ctx.shape[1]
    depth = w_mod.shape[0]
    lay = _Layout(batch, t_ctx, t_lat, with_ctx=True)
    lay_lat = _Layout(batch, t_ctx, t_lat, with_ctx=False)

    n_cond = 1 + batch
    pad = (-n_cond) % V7X_SUBLANES
    cond = jnp.concatenate([c_ctx[None], c, jnp.zeros((pad, d), F32)], axis=0)
    mods_all = _ada(cond, w_mod, b_mod)

    pos = _grid_pos_embed(t_lat // GRID_W, d)
    xall = _addpos(lay, ctx, x, pos)

    gd = d // FN_GROUPS
    cg, sg = _dft_tables(gd)
    cs_tab = jnp.asarray(np.concatenate([cg, sg], axis=1)).astype(BF16)
    ct_lat, st_lat = (jnp.asarray(a).astype(BF16) for a in _dft_tables(t_lat))
    ct_ctx, st_ctx = (jnp.asarray(a).astype(BF16) for a in _dft_tables(t_ctx))

    for i in range(depth):
        last = i == depth - 1
        j = i // N_MIXERS
        table = mods_all[i]
        mods = jnp.stack([jnp.broadcast_to(table[0:1], (lay.R, N_MOD * d)),
                          jnp.tile(table[1:1 + batch], (lay.TT, 1))])
        g1n = norm1_g[i][None]
        g2n = norm2_g[i][None]
        if i % N_MIXERS == 0:
            cur = lay
            gate, rec = _rg1(cur, xall, g1n, mods, rg_w_in[j].astype(BF16))
            hs = []
            for dr in range(2):
                hs.append(_rg2(cur, rec, rg_conv_w[j], rg_conv_b[j][None],
                               rg_w_a[j, dr].astype(BF16), rg_w_i[j, dr].astype(BF16),
                               rg_b_a[j, dr][None], rg_b_i[j, dr][None], rg_lam[j, dr][None],
                               rev=dr == 1))
            xall = _proj_res(cur, [hs[0], hs[1], gate], xall, mods, rg_w_out[j].astype(BF16))
        else:
            cur = lay_lat if last else lay
            x_off = lay.nct if last else 0
            zc, zs = _f1(lay, xall, g1n, mods, cs_tab, lay.nct, lay.nlt, latent=True)
            fre = [_f2(zc, zs, ct_lat, st_lat)]
            if not last:
                zc, zs = _f1(lay, xall, g1n, mods, cs_tab, 0, lay.nct, latent=False)
                fre = [_f2(zc, zs, ct_ctx, st_ctx)] + fre
            xall = _proj_res(cur, fre, xall, mods, fn_w_out[j].astype(BF16), x_off=x_off)
        if last and cur is lay:
            xall = xall[lay.nct * lay.R:]
            cur = lay_lat
        xall = _moe(cur, xall, g2n, mods, router_w[i].T, router_b[i][:, None],
                    exp_w_gate, exp_w_up, exp_w_down, i, sh_w_gate[i], sh_w_up[i], sh_w_down[i])

    return _final_norm(lay_lat, xall, final_norm_g[None])
```

```python
import functools
import math

import numpy as np
import jax
import jax.numpy as jnp
from jax import lax
from jax.experimental import pallas as pl
from jax.experimental.pallas import tpu as pltpu

GRID_W = 64
N_MIXERS = 2
LRU_HEADS = 8
CONV_W = 4
CONV_LEFT = 2
RG_C = 8.0
FN_GROUPS = 8
TOP_K = 8
N_GROUPS = 8
TOPK_GROUPS = 4
ROUTED_SCALE = 2.5
N_MOD = 6
EPS = 1e-6
POS_THETA = 10000.0

V7X_VMEM_BYTES = 64 * 1024 * 1024
V7X_SUBLANES = 8
V7X_LANES = 128
V7X_BF16_ROWS = 16
V7X_MXU_DIM = 256
MAX_ROW_TILE = 256
EXPERT_TILE_ROWS = 1024
EXPERT_SUBTILES = 2
VMEM_LIMIT = 56 * 1024 * 1024

F32 = jnp.float32
BF16 = jnp.bfloat16


def _params(sem, vmem=VMEM_LIMIT):
    return pltpu.CompilerParams(dimension_semantics=sem, vmem_limit_bytes=vmem)


def _resident(shape, index_map):
    return pl.BlockSpec(shape, index_map, pipeline_mode=pl.Buffered(1))


def _col_chunk(n, cap=512):
    c = min(n, cap)
    while n % c:
        c //= 2
    return c


def _normmod(x, g, shift, scale):
    ms = jnp.mean(x * x, axis=-1, keepdims=True)
    y = x * lax.rsqrt(ms + EPS) * g
    return y * (1.0 + scale) + shift


def _normmod_tiles(x_ref, g_ref, sh_ref, sc_ref, r):
    parts = [_normmod(x_ref[k * r:(k + 1) * r, :], g_ref[...], sh_ref[...], sc_ref[...])
             for k in range(x_ref.shape[0] // r)]
    return parts[0] if len(parts) == 1 else jnp.concatenate(parts, axis=0)


def _nt_dot(a, b):
    return lax.dot_general(a, b, (((1,), (1,)), ((), ())), preferred_element_type=F32)


def _split_by_batch(tile, s3, batch):
    r, d = tile.shape
    tt = r // batch
    nch = d // V7X_LANES
    for c in range(nch):
        s3[c] = tile[:, c * V7X_LANES:(c + 1) * V7X_LANES]
    return [jnp.concatenate([s3[c, pl.ds(b, tt, stride=batch), :] for c in range(nch)], axis=1)
            for b in range(batch)]


def _merge_by_batch(pieces, s3):
    batch = len(pieces)
    tt, d = pieces[0].shape
    nch = d // V7X_LANES
    for b, piece in enumerate(pieces):
        for c in range(nch):
            s3[c, pl.ds(b, tt, stride=batch), :] = piece[:, c * V7X_LANES:(c + 1) * V7X_LANES]
    return jnp.concatenate([s3[c] for c in range(nch)], axis=1)


class _Layout:
    def __init__(self, batch, t_ctx, t_lat, with_ctx=True):
        assert batch % V7X_SUBLANES == 0
        g = math.gcd(t_ctx, t_lat)
        fits = [tt for tt in range(1, g + 1)
                if g % tt == 0 and tt * batch <= MAX_ROW_TILE and (tt * batch) % V7X_BF16_ROWS == 0]
        tt = max(fits)
        self.TT = tt
        self.R = tt * batch
        self.B = batch
        self.Tc = t_ctx
        self.T = t_lat
        self.nct = t_ctx // tt if with_ctx else 0
        self.nlt = t_lat // tt
        self.ntiles = self.nct + self.nlt
        self.N = self.ntiles * self.R
        self.tiles_per_step = 2 if self.nct % 2 == 0 and self.nlt % 2 == 0 else 1


def _mod_spec(lay, d, which, tiles_per_step=1):
    return pl.BlockSpec((None, lay.R, d),
                        lambda i, *_: (jnp.where(i * tiles_per_step < lay.nct, 0, 1), 0, which))


def _ada_kernel(cond_ref, w_ref, b_ref, out_ref):
    s = jax.nn.silu(cond_ref[...]).astype(BF16)
    out_ref[...] = jnp.dot(s, w_ref[...].astype(BF16), preferred_element_type=F32) + b_ref[...]


def _ada(cond, w_mod, b_mod):
    depth, d, nm = w_mod.shape
    rows = cond.shape[0]
    tn = _col_chunk(nm, 1024)
    return pl.pallas_call(
        _ada_kernel,
        grid=(depth, nm // tn),
        in_specs=[
            pl.BlockSpec((rows, d), lambda l, j: (0, 0)),
            pl.BlockSpec((None, d, tn), lambda l, j: (l, 0, j)),
            pl.BlockSpec((None, 1, tn), lambda l, j: (l, 0, j)),
        ],
        out_specs=pl.BlockSpec((None, rows, tn), lambda l, j: (l, 0, j)),
        out_shape=jax.ShapeDtypeStruct((depth, rows, nm), F32),
        compiler_params=_params(("arbitrary", "arbitrary")),
        name="ada_mod",
    )(cond, w_mod, b_mod.reshape(depth, 1, nm))


def _addpos_kernel(ctx_ref, x_ref, pos_ref, out_ref, s3, *, nct, batch):
    i = pl.program_id(0)

    @pl.when(i < nct)
    def _():
        out_ref[...] = _merge_by_batch([ctx_ref[b] for b in range(batch)], s3)

    @pl.when(i >= nct)
    def _():
        out_ref[...] = _merge_by_batch([x_ref[b] + pos_ref[...] for b in range(batch)], s3)


def _addpos(lay, ctx, x, pos):
    batch, _, d = x.shape
    r, tt = lay.R, lay.TT
    return pl.pallas_call(
        functools.partial(_addpos_kernel, nct=lay.nct, batch=batch),
        grid=(lay.ntiles,),
        in_specs=[
            pl.BlockSpec((batch, tt, d), lambda i: (0, jnp.minimum(i, lay.nct - 1), 0)),
            pl.BlockSpec((batch, tt, d), lambda i: (0, jnp.maximum(i - lay.nct, 0), 0)),
            pl.BlockSpec((tt, d), lambda i: (jnp.maximum(i - lay.nct, 0), 0)),
        ],
        out_specs=pl.BlockSpec((r, d), lambda i: (i, 0)),
        out_shape=jax.ShapeDtypeStruct((lay.N, d), F32),
        scratch_shapes=[pltpu.VMEM((d // V7X_LANES, r, V7X_LANES), F32)],
        compiler_params=_params(("arbitrary",)),
        name="add_pos",
    )(ctx, x, pos)


def _rg1_kernel(x_ref, g_ref, sh_ref, sc_ref, w_ref, gate_ref, rec_ref, *, width, cn, r):
    h = _normmod_tiles(x_ref, g_ref, sh_ref, sc_ref, r).astype(BF16)
    for c in range(2 * width // cn):
        z = jnp.dot(h, w_ref[:, c * cn:(c + 1) * cn], preferred_element_type=F32)
        if c * cn < width:
            gate_ref[:, c * cn:(c + 1) * cn] = jax.nn.gelu(z).astype(BF16)
        else:
            rec_ref[:, c * cn - width:(c + 1) * cn - width] = z.astype(BF16)


def _rg1(lay, x, g, mods, w_in):
    d = x.shape[1]
    width = w_in.shape[1] // 2
    r = lay.R
    k = lay.tiles_per_step
    cn = _col_chunk(width)
    row = lambda i: (i, 0)
    return pl.pallas_call(
        functools.partial(_rg1_kernel, width=width, cn=cn, r=r),
        grid=(lay.ntiles // k,),
        in_specs=[
            pl.BlockSpec((k * r, d), row),
            _resident((1, d), lambda i: (0, 0)),
            _mod_spec(lay, d, 0, k),
            _mod_spec(lay, d, 1, k),
            _resident((d, 2 * width), lambda i: (0, 0)),
        ],
        out_specs=[pl.BlockSpec((k * r, width), row), pl.BlockSpec((k * r, width), row)],
        out_shape=[jax.ShapeDtypeStruct((lay.N, width), BF16)] * 2,
        compiler_params=_params(("arbitrary",)),
        name="rg_in_proj",
    )(x, g, mods, mods, w_in)


def _log_sigmoid(x):
    return jnp.minimum(x, 0.0) - jnp.log1p(jnp.exp(-jnp.abs(x)))


def _neg_expm1(t, exp_t):
    series = t * (1.0 + t * (0.5 + t * (1.0 / 6.0 + t * (1.0 / 24.0 + t * (1.0 / 120.0)))))
    return -jnp.where(t > -0.1, series, exp_t - 1.0)


def _rg2_kernel(cur_ref, prev_ref, next_ref, cw_ref, cb_ref, wa_ref, wi_ref, ba_ref, bi_ref,
                lam_ref, out_ref, carry_ref, *, rev, nct, nlt, tt, batch, hd, heads):
    s = pl.program_id(0)

    @pl.when(s == 0)
    def _():
        carry_ref[...] = jnp.zeros_like(carry_ref)

    in_ctx = s < nct
    pos = jnp.where(in_ctx, s, s - nct)
    n_seq = jnp.where(in_ctx, nct, nlt)
    chunk = (n_seq - 1 - pos) if rev else pos
    keep_prev = jnp.where(chunk == 0, 0.0, 1.0).astype(F32)
    keep_next = jnp.where(chunk == n_seq - 1, 0.0, 1.0).astype(F32)
    r = tt * batch

    for h in range(heads):
        sl = slice(h * hd, (h + 1) * hd)
        cur = cur_ref[:, sl].astype(F32)
        pv = prev_ref[:, sl].astype(F32) * keep_prev
        nx = next_ref[:, sl].astype(F32) * keep_next
        ext = jnp.concatenate([pv, cur, nx], axis=0)
        u = cb_ref[:, sl] + cw_ref[0:1, sl] * ext[0:r]
        for k in range(1, CONV_W):
            u = u + cw_ref[k:k + 1, sl] * ext[k * batch:k * batch + r]
        ub = u.astype(BF16)
        ra = jnp.dot(ub, wa_ref[h], preferred_element_type=F32) + ba_ref[:, sl]
        ri = jnp.dot(ub, wi_ref[h], preferred_element_type=F32) + bi_ref[:, sl]
        log_a = RG_C * jax.nn.sigmoid(ra) * _log_sigmoid(lam_ref[:, sl])
        a = jnp.exp(log_a)
        b = jnp.sqrt(_neg_expm1(2.0 * log_a, a * a)) * (jax.nn.sigmoid(ri) * u)

        state = carry_ref[:, sl]
        outs = [None] * tt
        for step in (range(tt - 1, -1, -1) if rev else range(tt)):
            rows = slice(step * batch, (step + 1) * batch)
            state = a[rows] * state + b[rows]
            outs[step] = state
        carry_ref[:, sl] = state
        out_ref[:, sl] = jnp.concatenate(outs, axis=0).astype(BF16)


def _rg2(lay, rec, conv_w, conv_b, w_a, w_i, b_a, b_i, lam, rev):
    n, width = rec.shape
    r, nct, nlt, batch = lay.R, lay.nct, lay.nlt, lay.B
    heads = w_a.shape[0]
    hd = width // heads
    left = CONV_LEFT * batch
    right = (CONV_W - 1 - CONV_LEFT) * batch
    assert r % left == 0 and r % right == 0

    def tile(s):
        if rev:
            return jnp.where(s < nct, nct - 1 - s, nct + nlt - 1 - (s - nct))
        return s

    const2 = lambda s: (0, 0)
    const3 = lambda s: (0, 0, 0)
    return pl.pallas_call(
        functools.partial(_rg2_kernel, rev=rev, nct=nct, nlt=nlt, tt=lay.TT, batch=batch, hd=hd, heads=heads),
        grid=(nct + nlt,),
        in_specs=[
            pl.BlockSpec((r, width), lambda s: (tile(s), 0)),
            pl.BlockSpec((left, width), lambda s: (jnp.maximum(tile(s) * (r // left) - 1, 0), 0)),
            pl.BlockSpec((right, width), lambda s: (jnp.minimum((tile(s) + 1) * (r // right), n // right - 1), 0)),
            _resident((CONV_W, width), const2),
            _resident((1, width), const2),
            _resident((heads, hd, hd), const3),
            _resident((heads, hd, hd), const3),
            _resident((1, width), const2),
            _resident((1, width), const2),
            _resident((1, width), const2),
        ],
        out_specs=pl.BlockSpec((r, width), lambda s: (tile(s), 0)),
        out_shape=jax.ShapeDtypeStruct((n, width), BF16),
        scratch_shapes=[pltpu.VMEM((batch, width), F32)],
        compiler_params=_params(("arbitrary",)),
        name="rg_scan_bwd" if rev else "rg_scan_fwd",
    )(rec, rec, rec, conv_w, conv_b, w_a, w_i, b_a, b_i, lam)


def _proj_res_kernel(*refs, mode, nct, cn, batch):
    if mode == "gated":
        hf_ref, hb_ref, gate_ref, x_ref, g1_ref, w_ref, out_ref = refs
        a = ((hf_ref[...].astype(F32) + hb_ref[...].astype(F32)) * gate_ref[...].astype(F32)).astype(BF16)
    else:
        if mode == "ctx_lat":
            ac_ref, al_ref, x_ref, g1_ref, w_ref, out_ref, a_s = refs
            blk = jnp.where(pl.program_id(0) < nct, ac_ref[...], al_ref[...])
        else:
            a_ref, x_ref, g1_ref, w_ref, out_ref, a_s = refs
            blk = a_ref[...]
        k = blk.shape[1] // batch
        a = _merge_by_batch([blk[:, b * k:(b + 1) * k].astype(F32) for b in range(batch)], a_s).astype(BF16)
    d = out_ref.shape[1]
    for c in range(d // cn):
        cs = slice(c * cn, (c + 1) * cn)
        y = jnp.dot(a, w_ref[:, cs], preferred_element_type=F32)
        out_ref[:, cs] = x_ref[:, cs] + g1_ref[:, cs] * y


def _proj_res(lay, a_list, x, mods, w_out, x_off=0):
    k, d = w_out.shape
    r = lay.R
    cn = _col_chunk(d)
    row = lambda i: (i, 0)
    mode = {1: "plain", 2: "ctx_lat", 3: "gated"}[len(a_list)]
    if mode == "ctx_lat":
        a_maps = [lambda i: (jnp.minimum(i, lay.nct - 1), 0), lambda i: (jnp.maximum(i - lay.nct, 0), 0)]
    else:
        a_maps = [row] * len(a_list)
    a_block = (r, k) if mode == "gated" else (lay.TT, lay.B * k)
    scratch = [] if mode == "gated" else [pltpu.VMEM((k // V7X_LANES, r, V7X_LANES), F32)]
    return pl.pallas_call(
        functools.partial(_proj_res_kernel, mode=mode, nct=lay.nct, cn=cn, batch=lay.B),
        grid=(lay.ntiles,),
        in_specs=[pl.BlockSpec(a_block, m) for m in a_maps] + [
            pl.BlockSpec((r, d), lambda i: (i + x_off, 0)),
            _mod_spec(lay, d, 2),
            _resident((k, d), lambda i: (0, 0)),
        ],
        out_specs=pl.BlockSpec((r, d), row),
        out_shape=jax.ShapeDtypeStruct((lay.N, d), F32),
        scratch_shapes=scratch,
        compiler_params=_params(("arbitrary",)),
        name="mixer_out_proj",
    )(*a_list, x, mods, w_out)


def _dft_tables(n):
    k = np.arange(n, dtype=np.int64)
    ang = 2.0 * np.pi * ((k[:, None] * k[None, :]) % n).astype(np.float64) / n
    s = 1.0 / np.sqrt(n)
    return (np.cos(ang) * s).astype(np.float32), (np.sin(ang) * s).astype(np.float32)


def _dft_tables_split(n):
    c, s = _dft_tables(n)
    h = n // 2
    return [jnp.asarray(a).astype(BF16) for a in (c[:h, 0::2], s[:h, 0::2], c[:h, 1::2], s[:h, 1::2])]


def _f1_kernel(x_ref, g_ref, sh_ref, sc_ref, cs_ref, perm_ref, zce_ref, zse_ref, zco_ref, zso_ref,
               *, groups, gd, batch, r):
    h = _normmod_tiles(x_ref, g_ref, sh_ref, sc_ref, r).astype(BF16)
    zs = [jnp.dot(h[:, gi * gd:(gi + 1) * gd], cs_ref[...], preferred_element_type=F32).astype(BF16)
          for gi in range(groups)]
    d = groups * gd
    half_t = x_ref.shape[0] // batch // 2
    for part, refs in ((slice(0, gd), (zce_ref, zco_ref)), (slice(gd, 2 * gd), (zse_ref, zso_ref))):
        tile = jnp.concatenate([z[:, part] for z in zs], axis=1)
        moved = jnp.dot(perm_ref[...], tile, preferred_element_type=F32).astype(BF16)
        for parity, out_ref in enumerate(refs):
            for b in range(batch):
                lo = (parity * batch + b) * half_t
                out_ref[:, b * d:(b + 1) * d] = moved[lo:lo + half_t, :]


def _f1(lay, x, g, mods, cs, first_tile, n_tiles, latent):
    d = x.shape[1]
    r, tt, batch = lay.R, lay.TT, lay.B
    assert first_tile % 2 == 0 and n_tiles % 2 == 0
    gd = d // FN_GROUPS
    row = lambda i: (i, 0)
    mod = lambda which: pl.BlockSpec((None, r, d), lambda i: (int(latent), 0, which))
    zspec = pl.BlockSpec((tt, batch * d), row)
    zshape = jax.ShapeDtypeStruct((n_tiles // 2 * tt, batch * d), BF16)
    q = np.arange(2 * r)
    parity, b_of, i_of = q // (batch * tt), (q // tt) % batch, q % tt
    perm = np.zeros((2 * r, 2 * r), np.float32)
    perm[q, (2 * i_of + parity) * batch + b_of] = 1.0
    return pl.pallas_call(
        functools.partial(_f1_kernel, groups=FN_GROUPS, gd=gd, batch=batch, r=r),
        grid=(n_tiles // 2,),
        in_specs=[
            pl.BlockSpec((2 * r, d), lambda i: (i + first_tile // 2, 0)),
            _resident((1, d), lambda i: (0, 0)),
            mod(0),
            mod(1),
            _resident((gd, 2 * gd), lambda i: (0, 0)),
            _resident((2 * r, 2 * r), lambda i: (0, 0)),
        ],
        out_specs=[zspec] * 4,
        out_shape=[zshape] * 4,
        compiler_params=_params(("arbitrary",)),
        name="fourier_channel_dft",
    )(x, g, mods, mods, cs, jnp.asarray(perm).astype(BF16))


def _f2_kernel(ce_ref, se_ref, co_ref, so_ref, zce_ref, zse_ref, zco_ref, zso_ref, out_ref, *, half, rc):
    for c in range(half // rc):
        rs = slice(c * rc, (c + 1) * rc)
        even = jnp.dot(ce_ref[rs, :], zce_ref[...], preferred_element_type=F32)
        even = even - jnp.dot(se_ref[rs, :], zse_ref[...], preferred_element_type=F32)
        odd = jnp.dot(co_ref[rs, :], zco_ref[...], preferred_element_type=F32)
        odd = odd - jnp.dot(so_ref[rs, :], zso_ref[...], preferred_element_type=F32)
        out_ref[c * rc:(c + 1) * rc, :] = (even + odd).astype(BF16)
        out_ref[half + c * rc:half + (c + 1) * rc, :] = (even - odd).astype(BF16)


def _f2(z4, tables):
    half, cols = z4[0].shape
    tn = _col_chunk(cols)
    rc = _col_chunk(half)
    zmap = lambda j: (0, j)
    return pl.pallas_call(
        functools.partial(_f2_kernel, half=half, rc=rc),
        grid=(cols // tn,),
        in_specs=[_resident((half, half), lambda j: (0, 0))] * 4 + [pl.BlockSpec((half, tn), zmap)] * 4,
        out_specs=pl.BlockSpec((2 * half, tn), zmap),
        out_shape=jax.ShapeDtypeStruct((2 * half, cols), BF16),
        compiler_params=_params(("arbitrary",)),
        name="fourier_time_dft",
    )(*tables, *z4)


def _pack_pairs(v):
    half = v.shape[1] // 2
    bits = pltpu.bitcast(v.astype(BF16).astype(F32), jnp.uint32)
    return (bits[:, :half] >> 16) | (bits[:, half:] & jnp.uint32(0xFFFF0000))


def _unpack_pairs(w):
    lo = pltpu.bitcast(w << 16, F32)
    hi = pltpu.bitcast(w & jnp.uint32(0xFFFF0000), F32)
    return lo, hi


def _router_kernel(x_ref, g_ref, sh_ref, sc_ref, rw_ref, rb_ref, tri_e_ref, tri_t_ref,
                   f_ref, ek_ref, rk_ref, wk_ref, cnt_ref, base_ref, *, n_exp):
    @pl.when(pl.program_id(0) == 0)
    def _():
        base_ref[...] = jnp.zeros_like(base_ref)

    f = _normmod(x_ref[...], g_ref[...], sh_ref[...], sc_ref[...])
    f_hi = f.astype(BF16)
    f_lo = (f - f_hi.astype(F32)).astype(BF16)
    f_ref[...] = _pack_pairs(f)
    rw = rw_ref[...]
    rw_hi = rw.astype(BF16)
    rw_lo = (rw - rw_hi.astype(F32)).astype(BF16)
    logits = _nt_dot(rw_hi, f_hi) + (_nt_dot(rw_hi, f_lo) + _nt_dot(rw_lo, f_hi))
    scores = jax.nn.sigmoid(logits)
    biased = scores + rb_ref[...]
    r = scores.shape[1]
    gsz = n_exp // N_GROUPS
    sub = lax.broadcasted_iota(jnp.int32, (gsz, r), 0)
    neg = jnp.float32(-jnp.inf)

    blocks = [biased[gi * gsz:(gi + 1) * gsz] for gi in range(N_GROUPS)]
    gscore = []
    for blk in blocks:
        m1 = jnp.max(blk, axis=0, keepdims=True)
        first = jnp.min(jnp.where(blk == m1, sub, gsz), axis=0, keepdims=True)
        m2 = jnp.max(jnp.where(sub == first, neg, blk), axis=0, keepdims=True)
        gscore.append(m1 + m2)
    masked = []
    for gi in range(N_GROUPS):
        rank = jnp.zeros((1, r), jnp.int32)
        for gj in range(N_GROUPS):
            if gj == gi:
                continue
            ahead = (gscore[gj] >= gscore[gi]) if gj < gi else (gscore[gj] > gscore[gi])
            rank = rank + ahead.astype(jnp.int32)
        masked.append(jnp.where(rank < TOPK_GROUPS, blocks[gi], neg))
    e_idx = [sub + gi * gsz for gi in range(N_GROUPS)]
    live = list(masked)
    picked = [jnp.zeros((gsz, r), jnp.bool_) for _ in range(N_GROUPS)]
    for _ in range(TOP_K):
        best = live[0]
        for gi in range(1, N_GROUPS):
            best = jnp.maximum(best, live[gi])
        best = jnp.max(best, axis=0, keepdims=True)
        first = jnp.where(live[0] == best, e_idx[0], n_exp)
        for gi in range(1, N_GROUPS):
            first = jnp.minimum(first, jnp.where(live[gi] == best, e_idx[gi], n_exp))
        first = jnp.min(first, axis=0, keepdims=True)
        for gi in range(N_GROUPS):
            hit = e_idx[gi] == first
            picked[gi] = picked[gi] | hit
            live[gi] = jnp.where(hit, neg, live[gi])
    sel_f = [jnp.where(pk, 1.0, 0.0) for pk in picked]
    sel_w = [jnp.where(pk, scores[gi * gsz:(gi + 1) * gsz], 0.0) for gi, pk in enumerate(picked)]
    tot = sel_w[0]
    for gi in range(1, N_GROUPS):
        tot = tot + sel_w[gi]
    denom = jnp.sum(tot, axis=0, keepdims=True)
    gates = jnp.concatenate(sel_w, axis=0) / denom * ROUTED_SCALE
    sel = jnp.concatenate(sel_f, axis=0)
    sel_b = sel.astype(BF16)
    slot = jnp.dot(tri_e_ref[...], sel_b, preferred_element_type=F32)
    base = base_ref[...]
    pos_in_expert = base + jnp.dot(sel_b, tri_t_ref[...], preferred_element_type=F32)
    base = base + jnp.sum(sel, axis=1, keepdims=True)
    base_ref[...] = base
    cnt_ref[...] = base
    e_idx = lax.broadcasted_iota(jnp.int32, sel.shape, 0).astype(F32)
    for k in range(TOP_K):
        hit = (sel > 0.0) & (slot == float(k))
        ek_ref[k:k + 1, :] = jnp.sum(jnp.where(hit, e_idx, 0.0), axis=0, keepdims=True).astype(jnp.int32)
        rk_ref[k:k + 1, :] = jnp.sum(jnp.where(hit, pos_in_expert, 0.0), axis=0, keepdims=True).astype(jnp.int32)
        wk_ref[k:k + 1, :] = jnp.sum(jnp.where(hit, gates, 0.0), axis=0, keepdims=True)


def _router(lay, x, g, mods, rw_t, rb):
    d = x.shape[1]
    n_exp = rw_t.shape[0]
    r = lay.R
    row = lambda i: (i, 0)
    tok = lambda i: (0, i)
    const = lambda i: (0, 0)
    tri_e = jnp.asarray(np.tril(np.ones((n_exp, n_exp), np.float32), -1)).astype(BF16)
    tri_t = jnp.asarray(np.triu(np.ones((r, r), np.float32), 1)).astype(BF16)
    return pl.pallas_call(
        functools.partial(_router_kernel, n_exp=n_exp),
        grid=(lay.ntiles,),
        in_specs=[
            pl.BlockSpec((r, d), row),
            _resident((1, d), const),
            _mod_spec(lay, d, 3),
            _mod_spec(lay, d, 4),
            _resident((n_exp, d), const),
            _resident((n_exp, 1), const),
            _resident((n_exp, n_exp), const),
            _resident((r, r), const),
        ],
        out_specs=[
            pl.BlockSpec((r, d // 2), row),
            pl.BlockSpec((TOP_K, r), tok),
            pl.BlockSpec((TOP_K, r), tok),
            pl.BlockSpec((TOP_K, r), tok),
            pl.BlockSpec((n_exp, 1), const),
        ],
        out_shape=[
            jax.ShapeDtypeStruct((lay.N, d // 2), jnp.uint32),
            jax.ShapeDtypeStruct((TOP_K, lay.N), jnp.int32),
            jax.ShapeDtypeStruct((TOP_K, lay.N), jnp.int32),
            jax.ShapeDtypeStruct((TOP_K, lay.N), F32),
            jax.ShapeDtypeStruct((n_exp, 1), F32),
        ],
        scratch_shapes=[pltpu.VMEM((n_exp, 1), F32)],
        compiler_params=_params(("arbitrary",)),
        name="moe_router",
    )(x, g, mods, mods, rw_t, rb, tri_e, tri_t)


def _dispatch_kernel(pad_start_ref, pad_len_ref, nused_ref, p_ref, f_ref, xs_ref, sem, zeros_ref,
                     *, n_exp, r, tmx):
    i = pl.program_id(0)

    def scatter_rows(t, carry):
        for k in range(TOP_K):
            p = p_ref[t * TOP_K + k]
            pltpu.make_async_copy(f_ref.at[pl.ds(t, 1)], xs_ref.at[pl.ds(p, 1)], sem.at[0]).start()
        return carry

    lax.fori_loop(0, r, scatter_rows, 0)
    pltpu.make_async_copy(xs_ref.at[pl.ds(0, r * TOP_K)], xs_ref.at[pl.ds(0, r * TOP_K)], sem.at[0]).wait()

    @pl.when(i == pl.num_programs(0) - 1)
    def _():
        zeros_ref[...] = jnp.zeros_like(zeros_ref)

        def pad_copies(e, act):
            start = pad_start_ref[e]
            length = pad_len_ref[e]
            head = jnp.minimum(length, (-start) & (V7X_SUBLANES - 1))
            base = start + head

            def row(j, c):
                act(pltpu.make_async_copy(zeros_ref.at[pl.ds(0, 1)], xs_ref.at[pl.ds(start + j, 1)], sem.at[1]))
                return c

            def group(j, c):
                off = pl.multiple_of(base + j * V7X_SUBLANES, V7X_SUBLANES)
                act(pltpu.make_async_copy(zeros_ref.at[pl.ds(0, V7X_SUBLANES)],
                                          xs_ref.at[pl.ds(off, V7X_SUBLANES)], sem.at[2]))
                return c

            lax.fori_loop(0, head, row, 0)
            lax.fori_loop(0, (length - head) // V7X_SUBLANES, group, 0)

        def tile_copy(m, act):
            dst = xs_ref.at[pl.ds(pl.multiple_of(m * tmx, tmx), tmx)]
            act(pltpu.make_async_copy(zeros_ref, dst, sem.at[3]))

        n_tiles = xs_ref.shape[0] // tmx
        for act in (lambda cp: cp.start(), lambda cp: cp.wait()):
            lax.fori_loop(0, n_exp, lambda e, c, act=act: (pad_copies(e, act), c)[1], 0)
            lax.fori_loop(nused_ref[0], n_tiles, lambda m, c, act=act: (tile_copy(m, act), c)[1], 0)


def _dispatch(lay, fpk, p_flat, pad_start, pad_len, n_used, n_rows, tmx):
    n, half = fpk.shape
    r = lay.R
    n_exp = pad_start.shape[0]
    grid_spec = pltpu.PrefetchScalarGridSpec(
        num_scalar_prefetch=3,
        grid=(lay.ntiles,),
        in_specs=[
            pl.BlockSpec((r * TOP_K,), lambda i, *_: (i,), memory_space=pltpu.SMEM),
            pl.BlockSpec((r, half), lambda i, *_: (i, 0)),
        ],
        out_specs=pl.BlockSpec(memory_space=pl.ANY),
        scratch_shapes=[pltpu.SemaphoreType.DMA((4,)), pltpu.VMEM((tmx, half), jnp.uint32)],
    )
    return pl.pallas_call(
        functools.partial(_dispatch_kernel, n_exp=n_exp, r=r, tmx=tmx),
        grid_spec=grid_spec,
        out_shape=jax.ShapeDtypeStruct((n_rows, half), jnp.uint32),
        compiler_params=_params(("arbitrary",)),
        name="moe_dispatch",
    )(pad_start, pad_len, n_used, p_flat, fpk)


def _swiglu_packed(xw, wg_ref, wu_ref, wd_ref):
    half = xw.shape[1]
    lo, hi = _unpack_pairs(xw)
    lo = lo.astype(BF16)
    hi = hi.astype(BF16)

    def proj(w_ref):
        top = jnp.dot(lo, w_ref[:half, :], preferred_element_type=F32)
        return top + jnp.dot(hi, w_ref[half:, :], preferred_element_type=F32)

    hid = (jax.nn.silu(proj(wg_ref)) * proj(wu_ref)).astype(BF16)
    return jnp.dot(hid, wd_ref[...], preferred_element_type=F32)


def _grouped_kernel(te_ref, nused_ref, xs_ref, wg_ref, wu_ref, wd_ref, ys_ref, wg_b, wu_b, wd_b):
    m = pl.program_id(0)

    @pl.when((m == 0) | (te_ref[m] != te_ref[jnp.maximum(m - 1, 0)]))
    def _():
        wg_b[...] = wg_ref[...].astype(BF16)
        wu_b[...] = wu_ref[...].astype(BF16)
        wd_b[...] = wd_ref[...].astype(BF16)

    @pl.when(m < nused_ref[0])
    def _():
        rows = xs_ref.shape[0] // EXPERT_SUBTILES
        for c in range(EXPERT_SUBTILES):
            rs = slice(c * rows, (c + 1) * rows)
            ys_ref[rs, :] = _pack_pairs(_swiglu_packed(xs_ref[rs, :], wg_b, wu_b, wd_b))

    @pl.when(m >= nused_ref[0])
    def _():
        ys_ref[...] = jnp.zeros_like(ys_ref)


def _grouped(xs, wg, wu, wd, layer, tile_expert, n_used, tmx):
    n_rows, half = xs.shape
    _, _, d, fdim = wg.shape
    grid_spec = pltpu.PrefetchScalarGridSpec(
        num_scalar_prefetch=2,
        grid=(n_rows // tmx,),
        in_specs=[
            pl.BlockSpec((tmx, half), lambda m, te, nu: (jnp.minimum(m, nu[0] - 1), 0)),
            pl.BlockSpec((None, None, d, fdim), lambda m, te, nu: (layer, te[m], 0, 0)),
            pl.BlockSpec((None, None, d, fdim), lambda m, te, nu: (layer, te[m], 0, 0)),
            pl.BlockSpec((None, None, fdim, d), lambda m, te, nu: (layer, te[m], 0, 0)),
        ],
        out_specs=pl.BlockSpec((tmx, half), lambda m, te, nu: (m, 0)),
        scratch_shapes=[pltpu.VMEM((d, fdim), BF16), pltpu.VMEM((d, fdim), BF16), pltpu.VMEM((fdim, d), BF16)],
    )
    return pl.pallas_call(
        _grouped_kernel,
        grid_spec=grid_spec,
        out_shape=jax.ShapeDtypeStruct((n_rows, half), jnp.uint32),
        compiler_params=_params(("arbitrary",)),
        name="moe_experts",
    )(tile_expert, n_used, xs, wg, wu, wd)


def _combine_kernel(p0_ref, pn_ref, w_ref, f_ref, x_ref, g2_ref, swg_ref, swu_ref, swd_ref, ys_ref, out_ref,
                    buf_a, buf_b, sh_ref, sem, *, r):
    i = pl.program_id(0)
    last = pl.num_programs(0) - 1
    half = f_ref.shape[1]
    rows8 = V7X_SUBLANES

    def issue(p_ref, dst, dsem, t0, count):
        for j in range(count):
            t = t0 + j
            for k in range(TOP_K):
                p = p_ref[t * TOP_K + k]
                pltpu.make_async_copy(ys_ref.at[pl.ds(p, 1)], dst.at[k, pl.ds(t, 1)], dsem).start()

    @pl.when(i == 0)
    def _():
        lax.fori_loop(0, r, lambda t, c: (issue(p0_ref, buf_a, sem.at[0], t, 1), c)[1], 0)

    def run(cur, cur_sem, nxt, nxt_sem):
        pltpu.make_async_copy(cur, cur, cur_sem).wait()
        sh_ref[...] = _swiglu_packed(f_ref[...], swg_ref, swu_ref, swd_ref)

        def finish_rows(g):
            rows = pl.ds(pl.multiple_of(g * rows8, rows8), rows8)
            acc_lo = sh_ref[rows, :half]
            acc_hi = sh_ref[rows, half:]
            w = w_ref[rows, :]
            for k in range(TOP_K):
                lo, hi = _unpack_pairs(cur[k, rows, :])
                wcol = w[:, k:k + 1]
                acc_lo = acc_lo + wcol * lo
                acc_hi = acc_hi + wcol * hi
            out_ref[rows, :half] = x_ref[rows, :half] + g2_ref[rows, :half] * acc_lo
            out_ref[rows, half:] = x_ref[rows, half:] + g2_ref[rows, half:] * acc_hi

        @pl.when(i < last)
        def _():
            def body(g, c):
                issue(pn_ref, nxt, nxt_sem, pl.multiple_of(g * rows8, rows8), rows8)
                finish_rows(g)
                return c

            lax.fori_loop(0, r // rows8, body, 0)

        @pl.when(i == last)
        def _():
            lax.fori_loop(0, r // rows8, lambda g, c: (finish_rows(g), c)[1], 0)

    @pl.when(i % 2 == 0)
    def _():
        run(buf_a, sem.at[0], buf_b, sem.at[1])

    @pl.when(i % 2 == 1)
    def _():
        run(buf_b, sem.at[1], buf_a, sem.at[0])


def _combine(lay, ys, p_flat, w_tok, fpk, x, mods, swg, swu, swd):
    n, d = x.shape
    half = d // 2
    r = lay.R
    fdim = swd.shape[0]
    row = lambda i: (i, 0)
    const = lambda i: (0, 0)
    gather_buf = pltpu.VMEM((TOP_K, r, half), jnp.uint32)
    return pl.pallas_call(
        functools.partial(_combine_kernel, r=r),
        grid=(lay.ntiles,),
        in_specs=[
            pl.BlockSpec((r * TOP_K,), lambda i: (i,), memory_space=pltpu.SMEM),
            pl.BlockSpec((r * TOP_K,), lambda i: (jnp.minimum(i + 1, lay.ntiles - 1),), memory_space=pltpu.SMEM),
            pl.BlockSpec((r, TOP_K), row),
            pl.BlockSpec((r, half), row),
            pl.BlockSpec((r, d), row),
            _mod_spec(lay, d, 5),
            _resident((d, fdim), const),
            _resident((d, fdim), const),
            _resident((fdim, d), const),
            pl.BlockSpec(memory_space=pl.ANY),
        ],
        out_specs=pl.BlockSpec((r, d), row),
        out_shape=jax.ShapeDtypeStruct((n, d), F32),
        scratch_shapes=[gather_buf, gather_buf, pltpu.VMEM((r, d), F32), pltpu.SemaphoreType.DMA((2,))],
        compiler_params=_params(("arbitrary",)),
        name="moe_combine",
    )(p_flat, p_flat, w_tok, fpk, x, mods, swg, swu, swd, ys)


def _moe(lay, x, g2n, mods, rw_t, rb, wg, wu, wd, layer, swg, swu, swd):
    n_exp = rw_t.shape[0]
    tmx = EXPERT_TILE_ROWS // lay.R * lay.R if lay.R <= EXPERT_TILE_ROWS else lay.R
    fpk, ek, rk, wk, cnt = _router(lay, x, g2n, mods, rw_t, rb)
    cnt = cnt[:, 0].astype(jnp.int32)
    padded = (cnt + tmx - 1) // tmx * tmx
    offs = jnp.concatenate([jnp.zeros((1,), jnp.int32), jnp.cumsum(padded)])
    n_tiles = lay.N * TOP_K // tmx + n_exp
    e_ids = jnp.arange(n_exp, dtype=jnp.int32)[:, None, None]
    seg_start = jnp.sum(jnp.where(ek[None] > e_ids, padded[:, None, None], 0), axis=0)
    p_flat = (seg_start + rk).T.reshape(-1)
    w_tok = wk.T
    n_used = (offs[n_exp] // tmx).reshape(1)
    tile_start = jnp.minimum(jnp.arange(n_tiles, dtype=jnp.int32), n_used[0] - 1) * tmx
    tile_expert = jnp.sum((tile_start[:, None] >= offs[None, 1:]).astype(jnp.int32), axis=1)
    xs = _dispatch(lay, fpk, p_flat, offs[:-1] + cnt, padded - cnt, n_used, n_tiles * tmx, tmx)
    ys = _grouped(xs, wg, wu, wd, layer, tile_expert, n_used, tmx)
    return _combine(lay, ys, p_flat, w_tok, fpk, x, mods, swg.astype(BF16), swu.astype(BF16), swd.astype(BF16))


def _final_norm_kernel(x_ref, g_ref, out_ref, s3, *, batch):
    x = x_ref[...]
    ms = jnp.mean(x * x, axis=-1, keepdims=True)
    y = x * lax.rsqrt(ms + EPS) * g_ref[...]
    for b, piece in enumerate(_split_by_batch(y, s3, batch)):
        out_ref[b] = piece


def _final_norm(lay, x, g):
    n, d = x.shape
    r, tt, batch = lay.R, lay.TT, lay.B
    return pl.pallas_call(
        functools.partial(_final_norm_kernel, batch=batch),
        grid=(n // r,),
        in_specs=[pl.BlockSpec((r, d), lambda i: (i, 0)), _resident((1, d), lambda i: (0, 0))],
        out_specs=pl.BlockSpec((batch, tt, d), lambda i: (0, i, 0)),
        out_shape=jax.ShapeDtypeStruct((batch, n // batch, d), F32),
        scratch_shapes=[pltpu.VMEM((d // V7X_LANES, r, V7X_LANES), F32)],
        compiler_params=_params(("arbitrary",)),
        name="final_norm",
    )(x, g)


def _grid_pos_embed(rows, dim):
    row = np.repeat(np.arange(rows, dtype=np.float32), GRID_W)
    col = np.tile(np.arange(GRID_W, dtype=np.float32), rows)
    quarter = dim // 4
    freqs = jnp.float32(POS_THETA) ** (-jnp.arange(quarter, dtype=F32) / quarter)
    ar = jnp.asarray(row)[:, None] * freqs
    ac = jnp.asarray(col)[:, None] * freqs
    return jnp.concatenate([jnp.sin(ar), jnp.cos(ar), jnp.sin(ac), jnp.cos(ac)], axis=-1).astype(F32)


def kernel(x, c, ctx, c_ctx, norm1_g, norm2_g, final_norm_g, w_mod, b_mod, rg_w_in, rg_conv_w, rg_conv_b, rg_w_a, rg_b_a, rg_w_i, rg_b_i, rg_lam, rg_w_out, fn_w_out, router_w, router_b, exp_w_gate, exp_w_up, exp_w_down, sh_w_gate, sh_w_up, sh_w_down):
    batch, t_lat, d = x.shape
    t_ctx = ctx.shape[1]
    depth = w_mod.shape[0]
    lay = _Layout(batch, t_ctx, t_lat, with_ctx=True)
    lay_lat = _Layout(batch, t_ctx, t_lat, with_ctx=False)

    n_cond = 1 + batch
    pad = (-n_cond) % V7X_SUBLANES
    cond = jnp.concatenate([c_ctx[None], c, jnp.zeros((pad, d), F32)], axis=0)
    mods_all = _ada(cond, w_mod, b_mod)

    pos = _grid_pos_embed(t_lat // GRID_W, d)
    xall = _addpos(lay, ctx, x, pos)

    gd = d // FN_GROUPS
    cg, sg = _dft_tables(gd)
    cs_tab = jnp.asarray(np.concatenate([cg, sg], axis=1)).astype(BF16)
    tab_lat = _dft_tables_split(t_lat)
    tab_ctx = _dft_tables_split(t_ctx)

    for i in range(depth):
        last = i == depth - 1
        j = i // N_MIXERS
        table = mods_all[i]
        mods = jnp.stack([jnp.broadcast_to(table[0:1], (lay.R, N_MOD * d)),
                          jnp.tile(table[1:1 + batch], (lay.TT, 1))])
        g1n = norm1_g[i][None]
        g2n = norm2_g[i][None]
        if i % N_MIXERS == 0:
            cur = lay
            gate, rec = _rg1(cur, xall, g1n, mods, rg_w_in[j].astype(BF16))
            hs = []
            for dr in range(2):
                hs.append(_rg2(cur, rec, rg_conv_w[j], rg_conv_b[j][None],
                               rg_w_a[j, dr].astype(BF16), rg_w_i[j, dr].astype(BF16),
                               rg_b_a[j, dr][None], rg_b_i[j, dr][None], rg_lam[j, dr][None],
                               rev=dr == 1))
            xall = _proj_res(cur, [hs[0], hs[1], gate], xall, mods, rg_w_out[j].astype(BF16))
        else:
            cur = lay_lat if last else lay
            x_off = lay.nct if last else 0
            fre = [_f2(_f1(lay, xall, g1n, mods, cs_tab, lay.nct, lay.nlt, latent=True), tab_lat)]
            if not last:
                fre = [_f2(_f1(lay, xall, g1n, mods, cs_tab, 0, lay.nct, latent=False), tab_ctx)] + fre
            xall = _proj_res(cur, fre, xall, mods, fn_w_out[j].astype(BF16), x_off=x_off)
        if last and cur is lay:
            xall = xall[lay.nct * lay.R:]
            cur = lay_lat
        xall = _moe(cur, xall, g2n, mods, router_w[i].T, router_b[i][:, None],
                    exp_w_gate, exp_w_up, exp_w_down, i, sh_w_gate[i], sh_w_up[i], sh_w_down[i])

    return _final_norm(lay_lat, xall, final_norm_g[None])
```

```python
import functools
import math

import numpy as np
import jax
import jax.numpy as jnp
from jax import lax
from jax.experimental import pallas as pl
from jax.experimental.pallas import tpu as pltpu

GRID_W = 64
N_MIXERS = 2
LRU_HEADS = 8
CONV_W = 4
CONV_LEFT = 2
RG_C = 8.0
FN_GROUPS = 8
TOP_K = 8
N_GROUPS = 8
TOPK_GROUPS = 4
ROUTED_SCALE = 2.5
N_MOD = 6
EPS = 1e-6
POS_THETA = 10000.0

V7X_VMEM_BYTES = 64 * 1024 * 1024
V7X_SUBLANES = 8
V7X_LANES = 128
V7X_BF16_ROWS = 16
V7X_MXU_DIM = 256
MAX_ROW_TILE = 256
EXPERT_TILE_ROWS = 1024
EXPERT_SUBTILES = 2
VMEM_LIMIT = 56 * 1024 * 1024

F32 = jnp.float32
BF16 = jnp.bfloat16


def _params(sem, vmem=VMEM_LIMIT):
    return pltpu.CompilerParams(dimension_semantics=sem, vmem_limit_bytes=vmem)


def _resident(shape, index_map):
    return pl.BlockSpec(shape, index_map, pipeline_mode=pl.Buffered(1))


def _col_chunk(n, cap=512):
    c = min(n, cap)
    while n % c:
        c //= 2
    return c


def _normmod(x, g, shift, scale):
    ms = jnp.mean(x * x, axis=-1, keepdims=True)
    y = x * lax.rsqrt(ms + EPS) * g
    return y * (1.0 + scale) + shift


def _normmod_tiles(x_ref, g_ref, sh_ref, sc_ref, r):
    parts = [_normmod(x_ref[k * r:(k + 1) * r, :], g_ref[...], sh_ref[...], sc_ref[...])
             for k in range(x_ref.shape[0] // r)]
    return parts[0] if len(parts) == 1 else jnp.concatenate(parts, axis=0)


def _nt_dot(a, b):
    return lax.dot_general(a, b, (((1,), (1,)), ((), ())), preferred_element_type=F32)


def _split_by_batch(tile, s3, batch):
    r, d = tile.shape
    tt = r // batch
    nch = d // V7X_LANES
    for c in range(nch):
        s3[c] = tile[:, c * V7X_LANES:(c + 1) * V7X_LANES]
    return [jnp.concatenate([s3[c, pl.ds(b, tt, stride=batch), :] for c in range(nch)], axis=1)
            for b in range(batch)]


def _merge_by_batch(pieces, s3):
    batch = len(pieces)
    tt, d = pieces[0].shape
    nch = d // V7X_LANES
    for b, piece in enumerate(pieces):
        for c in range(nch):
            s3[c, pl.ds(b, tt, stride=batch), :] = piece[:, c * V7X_LANES:(c + 1) * V7X_LANES]
    return jnp.concatenate([s3[c] for c in range(nch)], axis=1)


class _Layout:
    def __init__(self, batch, t_ctx, t_lat, with_ctx=True):
        assert batch % V7X_SUBLANES == 0
        g = math.gcd(t_ctx, t_lat)
        fits = [tt for tt in range(1, g + 1)
                if g % tt == 0 and tt * batch <= MAX_ROW_TILE and (tt * batch) % V7X_BF16_ROWS == 0]
        tt = max(fits)
        self.TT = tt
        self.R = tt * batch
        self.B = batch
        self.Tc = t_ctx
        self.T = t_lat
        self.nct = t_ctx // tt if with_ctx else 0
        self.nlt = t_lat // tt
        self.ntiles = self.nct + self.nlt
        self.N = self.ntiles * self.R
        self.tiles_per_step = 2 if self.nct % 2 == 0 and self.nlt % 2 == 0 else 1


def _mod_spec(lay, d, which, tiles_per_step=1):
    return pl.BlockSpec((None, lay.R, d),
                        lambda i, *_: (jnp.where(i * tiles_per_step < lay.nct, 0, 1), 0, which))


def _ada_kernel(cond_ref, w_ref, b_ref, out_ref):
    s = jax.nn.silu(cond_ref[...]).astype(BF16)
    out_ref[...] = jnp.dot(s, w_ref[...].astype(BF16), preferred_element_type=F32) + b_ref[...]


def _ada(cond, w_mod, b_mod):
    depth, d, nm = w_mod.shape
    rows = cond.shape[0]
    tn = _col_chunk(nm, 1024)
    return pl.pallas_call(
        _ada_kernel,
        grid=(depth, nm // tn),
        in_specs=[
            pl.BlockSpec((rows, d), lambda l, j: (0, 0)),
            pl.BlockSpec((None, d, tn), lambda l, j: (l, 0, j)),
            pl.BlockSpec((None, 1, tn), lambda l, j: (l, 0, j)),
        ],
        out_specs=pl.BlockSpec((None, rows, tn), lambda l, j: (l, 0, j)),
        out_shape=jax.ShapeDtypeStruct((depth, rows, nm), F32),
        compiler_params=_params(("arbitrary", "arbitrary")),
        name="ada_mod",
    )(cond, w_mod, b_mod.reshape(depth, 1, nm))


def _addpos_kernel(ctx_ref, x_ref, pos_ref, out_ref, s3, *, nct, batch):
    i = pl.program_id(0)

    @pl.when(i < nct)
    def _():
        out_ref[...] = _merge_by_batch([ctx_ref[b] for b in range(batch)], s3)

    @pl.when(i >= nct)
    def _():
        out_ref[...] = _merge_by_batch([x_ref[b] + pos_ref[...] for b in range(batch)], s3)


def _addpos(lay, ctx, x, pos):
    batch, _, d = x.shape
    r, tt = lay.R, lay.TT
    return pl.pallas_call(
        functools.partial(_addpos_kernel, nct=lay.nct, batch=batch),
        grid=(lay.ntiles,),
        in_specs=[
            pl.BlockSpec((batch, tt, d), lambda i: (0, jnp.minimum(i, lay.nct - 1), 0)),
            pl.BlockSpec((batch, tt, d), lambda i: (0, jnp.maximum(i - lay.nct, 0), 0)),
            pl.BlockSpec((tt, d), lambda i: (jnp.maximum(i - lay.nct, 0), 0)),
        ],
        out_specs=pl.BlockSpec((r, d), lambda i: (i, 0)),
        out_shape=jax.ShapeDtypeStruct((lay.N, d), F32),
        scratch_shapes=[pltpu.VMEM((d // V7X_LANES, r, V7X_LANES), F32)],
        compiler_params=_params(("arbitrary",)),
        name="add_pos",
    )(ctx, x, pos)


def _rg1_kernel(x_ref, g_ref, sh_ref, sc_ref, w_ref, gate_ref, rec_ref, *, width, cn, r):
    h = _normmod_tiles(x_ref, g_ref, sh_ref, sc_ref, r).astype(BF16)
    for c in range(2 * width // cn):
        z = jnp.dot(h, w_ref[:, c * cn:(c + 1) * cn], preferred_element_type=F32)
        if c * cn < width:
            gate_ref[:, c * cn:(c + 1) * cn] = jax.nn.gelu(z).astype(BF16)
        else:
            rec_ref[:, c * cn - width:(c + 1) * cn - width] = z.astype(BF16)


def _rg1(lay, x, g, mods, w_in):
    d = x.shape[1]
    width = w_in.shape[1] // 2
    r = lay.R
    k = lay.tiles_per_step
    cn = _col_chunk(width)
    row = lambda i: (i, 0)
    return pl.pallas_call(
        functools.partial(_rg1_kernel, width=width, cn=cn, r=r),
        grid=(lay.ntiles // k,),
        in_specs=[
            pl.BlockSpec((k * r, d), row),
            _resident((1, d), lambda i: (0, 0)),
            _mod_spec(lay, d, 0, k),
            _mod_spec(lay, d, 1, k),
            _resident((d, 2 * width), lambda i: (0, 0)),
        ],
        out_specs=[pl.BlockSpec((k * r, width), row), pl.BlockSpec((k * r, width), row)],
        out_shape=[jax.ShapeDtypeStruct((lay.N, width), BF16)] * 2,
        compiler_params=_params(("arbitrary",)),
        name="rg_in_proj",
    )(x, g, mods, mods, w_in)


def _log_sigmoid(x):
    return jnp.minimum(x, 0.0) - jnp.log1p(jnp.exp(-jnp.abs(x)))


def _neg_expm1_twice(x, exp_x):
    return (-1.0 - exp_x * exp_x) * jnp.tanh(x)


def _rg2_kernel(cur_ref, prev_ref, next_ref, cw_ref, cb_ref, wa_ref, wi_ref, ba_ref, bi_ref,
                lam_ref, out_ref, carry_ref, *, rev, nct, nlt, tt, batch, hd, heads):
    s = pl.program_id(0)

    @pl.when(s == 0)
    def _():
        carry_ref[...] = jnp.zeros_like(carry_ref)

    in_ctx = s < nct
    pos = jnp.where(in_ctx, s, s - nct)
    n_seq = jnp.where(in_ctx, nct, nlt)
    chunk = (n_seq - 1 - pos) if rev else pos
    keep_prev = jnp.where(chunk == 0, 0.0, 1.0).astype(F32)
    keep_next = jnp.where(chunk == n_seq - 1, 0.0, 1.0).astype(F32)
    r = tt * batch

    for h in range(heads):
        sl = slice(h * hd, (h + 1) * hd)
        cur = cur_ref[:, sl].astype(F32)
        pv = prev_ref[:, sl].astype(F32) * keep_prev
        nx = next_ref[:, sl].astype(F32) * keep_next
        ext = jnp.concatenate([pv, cur, nx], axis=0)
        u = cb_ref[:, sl] + cw_ref[0:1, sl] * ext[0:r]
        for k in range(1, CONV_W):
            u = u + cw_ref[k:k + 1, sl] * ext[k * batch:k * batch + r]
        ub = u.astype(BF16)
        ra = jnp.dot(ub, wa_ref[h], preferred_element_type=F32) + ba_ref[:, sl]
        ri = jnp.dot(ub, wi_ref[h], preferred_element_type=F32) + bi_ref[:, sl]
        log_a = RG_C * jax.nn.sigmoid(ra) * _log_sigmoid(lam_ref[:, sl])
        a = jnp.exp(log_a)
        b = jnp.sqrt(_neg_expm1_twice(log_a, a)) * (jax.nn.sigmoid(ri) * u)

        state = carry_ref[:, sl]
        outs = [None] * tt
        for step in (range(tt - 1, -1, -1) if rev else range(tt)):
            rows = slice(step * batch, (step + 1) * batch)
            state = a[rows] * state + b[rows]
            outs[step] = state
        carry_ref[:, sl] = state
        out_ref[:, sl] = jnp.concatenate(outs, axis=0).astype(BF16)


def _rg2(lay, rec, conv_w, conv_b, w_a, w_i, b_a, b_i, lam, rev):
    n, width = rec.shape
    r, nct, nlt, batch = lay.R, lay.nct, lay.nlt, lay.B
    heads = w_a.shape[0]
    hd = width // heads
    left = CONV_LEFT * batch
    right = (CONV_W - 1 - CONV_LEFT) * batch
    assert r % left == 0 and r % right == 0

    def tile(s):
        if rev:
            return jnp.where(s < nct, nct - 1 - s, nct + nlt - 1 - (s - nct))
        return s

    const2 = lambda s: (0, 0)
    const3 = lambda s: (0, 0, 0)
    return pl.pallas_call(
        functools.partial(_rg2_kernel, rev=rev, nct=nct, nlt=nlt, tt=lay.TT, batch=batch, hd=hd, heads=heads),
        grid=(nct + nlt,),
        in_specs=[
            pl.BlockSpec((r, width), lambda s: (tile(s), 0)),
            pl.BlockSpec((left, width), lambda s: (jnp.maximum(tile(s) * (r // left) - 1, 0), 0)),
            pl.BlockSpec((right, width), lambda s: (jnp.minimum((tile(s) + 1) * (r // right), n // right - 1), 0)),
            _resident((CONV_W, width), const2),
            _resident((1, width), const2),
            _resident((heads, hd, hd), const3),
            _resident((heads, hd, hd), const3),
            _resident((1, width), const2),
            _resident((1, width), const2),
            _resident((1, width), const2),
        ],
        out_specs=pl.BlockSpec((r, width), lambda s: (tile(s), 0)),
        out_shape=jax.ShapeDtypeStruct((n, width), BF16),
        scratch_shapes=[pltpu.VMEM((batch, width), F32)],
        compiler_params=_params(("arbitrary",)),
        name="rg_scan_bwd" if rev else "rg_scan_fwd",
    )(rec, rec, rec, conv_w, conv_b, w_a, w_i, b_a, b_i, lam)


def _proj_res_kernel(*refs, mode, nct, cn, batch):
    if mode == "gated":
        hf_ref, hb_ref, gate_ref, x_ref, g1_ref, w_ref, out_ref = refs
        a = ((hf_ref[...].astype(F32) + hb_ref[...].astype(F32)) * gate_ref[...].astype(F32)).astype(BF16)
    else:
        if mode == "ctx_lat":
            ac_ref, al_ref, x_ref, g1_ref, w_ref, out_ref, a_s = refs
            blk = jnp.where(pl.program_id(0) < nct, ac_ref[...], al_ref[...])
        else:
            a_ref, x_ref, g1_ref, w_ref, out_ref, a_s = refs
            blk = a_ref[...]
        k = blk.shape[1] // batch
        a = _merge_by_batch([blk[:, b * k:(b + 1) * k].astype(F32) for b in range(batch)], a_s).astype(BF16)
    d = out_ref.shape[1]
    for c in range(d // cn):
        cs = slice(c * cn, (c + 1) * cn)
        y = jnp.dot(a, w_ref[:, cs], preferred_element_type=F32)
        out_ref[:, cs] = x_ref[:, cs] + g1_ref[:, cs] * y


def _proj_res(lay, a_list, x, mods, w_out, x_off=0):
    k, d = w_out.shape
    r = lay.R
    cn = _col_chunk(d)
    row = lambda i: (i, 0)
    mode = {1: "plain", 2: "ctx_lat", 3: "gated"}[len(a_list)]
    if mode == "ctx_lat":
        a_maps = [lambda i: (jnp.minimum(i, lay.nct - 1), 0), lambda i: (jnp.maximum(i - lay.nct, 0), 0)]
    else:
        a_maps = [row] * len(a_list)
    a_block = (r, k) if mode == "gated" else (lay.TT, lay.B * k)
    scratch = [] if mode == "gated" else [pltpu.VMEM((k // V7X_LANES, r, V7X_LANES), F32)]
    return pl.pallas_call(
        functools.partial(_proj_res_kernel, mode=mode, nct=lay.nct, cn=cn, batch=lay.B),
        grid=(lay.ntiles,),
        in_specs=[pl.BlockSpec(a_block, m) for m in a_maps] + [
            pl.BlockSpec((r, d), lambda i: (i + x_off, 0)),
            _mod_spec(lay, d, 2),
            _resident((k, d), lambda i: (0, 0)),
        ],
        out_specs=pl.BlockSpec((r, d), row),
        out_shape=jax.ShapeDtypeStruct((lay.N, d), F32),
        scratch_shapes=scratch,
        compiler_params=_params(("arbitrary",)),
        name="mixer_out_proj",
    )(*a_list, x, mods, w_out)


def _dft_tables(n):
    k = np.arange(n, dtype=np.int64)
    ang = 2.0 * np.pi * ((k[:, None] * k[None, :]) % n).astype(np.float64) / n
    s = 1.0 / np.sqrt(n)
    return (np.cos(ang) * s).astype(np.float32), (np.sin(ang) * s).astype(np.float32)


def _dft_tables_split(n):
    c, s = _dft_tables(n)
    h = n // 2
    return [jnp.asarray(a).astype(BF16) for a in (c[:h, 0::2], s[:h, 0::2], c[:h, 1::2], s[:h, 1::2])]


def _f1_kernel(x_ref, g_ref, sh_ref, sc_ref, cs_ref, perm_ref, zce_ref, zse_ref, zco_ref, zso_ref,
               *, groups, gd, batch, r):
    h = _normmod_tiles(x_ref, g_ref, sh_ref, sc_ref, r).astype(BF16)
    zs = [jnp.dot(h[:, gi * gd:(gi + 1) * gd], cs_ref[...], preferred_element_type=F32).astype(BF16)
          for gi in range(groups)]
    d = groups * gd
    half_t = x_ref.shape[0] // batch // 2
    for part, refs in ((slice(0, gd), (zce_ref, zco_ref)), (slice(gd, 2 * gd), (zse_ref, zso_ref))):
        tile = jnp.concatenate([z[:, part] for z in zs], axis=1)
        moved = jnp.dot(perm_ref[...], tile, preferred_element_type=F32).astype(BF16)
        for parity, out_ref in enumerate(refs):
            for b in range(batch):
                lo = (parity * batch + b) * half_t
                out_ref[:, b * d:(b + 1) * d] = moved[lo:lo + half_t, :]


def _f1(lay, x, g, mods, cs, first_tile, n_tiles, latent):
    d = x.shape[1]
    r, tt, batch = lay.R, lay.TT, lay.B
    assert first_tile % 2 == 0 and n_tiles % 2 == 0
    gd = d // FN_GROUPS
    row = lambda i: (i, 0)
    mod = lambda which: pl.BlockSpec((None, r, d), lambda i: (int(latent), 0, which))
    zspec = pl.BlockSpec((tt, batch * d), row)
    zshape = jax.ShapeDtypeStruct((n_tiles // 2 * tt, batch * d), BF16)
    q = np.arange(2 * r)
    parity, b_of, i_of = q // (batch * tt), (q // tt) % batch, q % tt
    perm = np.zeros((2 * r, 2 * r), np.float32)
    perm[q, (2 * i_of + parity) * batch + b_of] = 1.0
    return pl.pallas_call(
        functools.partial(_f1_kernel, groups=FN_GROUPS, gd=gd, batch=batch, r=r),
        grid=(n_tiles // 2,),
        in_specs=[
            pl.BlockSpec((2 * r, d), lambda i: (i + first_tile // 2, 0)),
            _resident((1, d), lambda i: (0, 0)),
            mod(0),
            mod(1),
            _resident((gd, 2 * gd), lambda i: (0, 0)),
            _resident((2 * r, 2 * r), lambda i: (0, 0)),
        ],
        out_specs=[zspec] * 4,
        out_shape=[zshape] * 4,
        compiler_params=_params(("arbitrary",)),
        name="fourier_channel_dft",
    )(x, g, mods, mods, cs, jnp.asarray(perm).astype(BF16))


def _f2_kernel(ce_ref, se_ref, co_ref, so_ref, zce_ref, zse_ref, zco_ref, zso_ref, out_ref, *, half, rc):
    for c in range(half // rc):
        rs = slice(c * rc, (c + 1) * rc)
        even = jnp.dot(ce_ref[rs, :], zce_ref[...], preferred_element_type=F32)
        even = even - jnp.dot(se_ref[rs, :], zse_ref[...], preferred_element_type=F32)
        odd = jnp.dot(co_ref[rs, :], zco_ref[...], preferred_element_type=F32)
        odd = odd - jnp.dot(so_ref[rs, :], zso_ref[...], preferred_element_type=F32)
        out_ref[c * rc:(c + 1) * rc, :] = (even + odd).astype(BF16)
        out_ref[half + c * rc:half + (c + 1) * rc, :] = (even - odd).astype(BF16)


def _f2(z4, tables):
    half, cols = z4[0].shape
    tn = _col_chunk(cols)
    rc = _col_chunk(half)
    zmap = lambda j: (0, j)
    return pl.pallas_call(
        functools.partial(_f2_kernel, half=half, rc=rc),
        grid=(cols // tn,),
        in_specs=[_resident((half, half), lambda j: (0, 0))] * 4 + [pl.BlockSpec((half, tn), zmap)] * 4,
        out_specs=pl.BlockSpec((2 * half, tn), zmap),
        out_shape=jax.ShapeDtypeStruct((2 * half, cols), BF16),
        compiler_params=_params(("arbitrary",)),
        name="fourier_time_dft",
    )(*tables, *z4)


def _pack_pairs(v):
    half = v.shape[1] // 2
    bits = pltpu.bitcast(v.astype(BF16).astype(F32), jnp.uint32)
    return (bits[:, :half] >> 16) | (bits[:, half:] & jnp.uint32(0xFFFF0000))


def _unpack_pairs(w):
    lo = pltpu.bitcast(w << 16, F32)
    hi = pltpu.bitcast(w & jnp.uint32(0xFFFF0000), F32)
    return lo, hi


def _router_kernel(x_ref, g_ref, sh_ref, sc_ref, rw_ref, rb_ref, tri_e_ref, tri_t_ref,
                   f_ref, ek_ref, rk_ref, wk_ref, cnt_ref, base_ref, *, n_exp):
    @pl.when(pl.program_id(0) == 0)
    def _():
        base_ref[...] = jnp.zeros_like(base_ref)

    f = _normmod(x_ref[...], g_ref[...], sh_ref[...], sc_ref[...])
    f_hi = f.astype(BF16)
    f_lo = (f - f_hi.astype(F32)).astype(BF16)
    f_ref[...] = _pack_pairs(f)
    rw = rw_ref[...]
    rw_hi = rw.astype(BF16)
    rw_lo = (rw - rw_hi.astype(F32)).astype(BF16)
    logits = _nt_dot(rw_hi, f_hi) + (_nt_dot(rw_hi, f_lo) + _nt_dot(rw_lo, f_hi))
    scores = jax.nn.sigmoid(logits)
    biased = scores + rb_ref[...]
    r = scores.shape[1]
    gsz = n_exp // N_GROUPS
    sub = lax.broadcasted_iota(jnp.int32, (gsz, r), 0)
    neg = jnp.float32(-jnp.inf)

    blocks = [biased[gi * gsz:(gi + 1) * gsz] for gi in range(N_GROUPS)]
    gscore = []
    for blk in blocks:
        m1 = jnp.max(blk, axis=0, keepdims=True)
        first = jnp.min(jnp.where(blk == m1, sub, gsz), axis=0, keepdims=True)
        m2 = jnp.max(jnp.where(sub == first, neg, blk), axis=0, keepdims=True)
        gscore.append(m1 + m2)
    masked = []
    for gi in range(N_GROUPS):
        rank = jnp.zeros((1, r), jnp.int32)
        for gj in range(N_GROUPS):
            if gj == gi:
                continue
            ahead = (gscore[gj] >= gscore[gi]) if gj < gi else (gscore[gj] > gscore[gi])
            rank = rank + ahead.astype(jnp.int32)
        masked.append(jnp.where(rank < TOPK_GROUPS, blocks[gi], neg))
    e_idx = [sub + gi * gsz for gi in range(N_GROUPS)]
    live = list(masked)
    picked = [jnp.zeros((gsz, r), jnp.bool_) for _ in range(N_GROUPS)]
    for _ in range(TOP_K):
        best = live[0]
        for gi in range(1, N_GROUPS):
            best = jnp.maximum(best, live[gi])
        best = jnp.max(best, axis=0, keepdims=True)
        first = jnp.where(live[0] == best, e_idx[0], n_exp)
        for gi in range(1, N_GROUPS):
            first = jnp.minimum(first, jnp.where(live[gi] == best, e_idx[gi], n_exp))
        first = jnp.min(first, axis=0, keepdims=True)
        for gi in range(N_GROUPS):
            hit = e_idx[gi] == first
            picked[gi] = picked[gi] | hit
            live[gi] = jnp.where(hit, neg, live[gi])
    sel_f = [jnp.where(pk, 1.0, 0.0) for pk in picked]
    sel_w = [jnp.where(pk, scores[gi * gsz:(gi + 1) * gsz], 0.0) for gi, pk in enumerate(picked)]
    tot = sel_w[0]
    for gi in range(1, N_GROUPS):
        tot = tot + sel_w[gi]
    denom = jnp.sum(tot, axis=0, keepdims=True)
    gates = jnp.concatenate(sel_w, axis=0) / denom * ROUTED_SCALE
    sel = jnp.concatenate(sel_f, axis=0)
    sel_b = sel.astype(BF16)
    slot = jnp.dot(tri_e_ref[...], sel_b, preferred_element_type=F32)
    base = base_ref[...]
    pos_in_expert = base + jnp.dot(sel_b, tri_t_ref[...], preferred_element_type=F32)
    base = base + jnp.sum(sel, axis=1, keepdims=True)
    base_ref[...] = base
    cnt_ref[...] = base
    e_idx = lax.broadcasted_iota(jnp.int32, sel.shape, 0).astype(F32)
    for k in range(TOP_K):
        hit = (sel > 0.0) & (slot == float(k))
        ek_ref[k:k + 1, :] = jnp.sum(jnp.where(hit, e_idx, 0.0), axis=0, keepdims=True).astype(jnp.int32)
        rk_ref[k:k + 1, :] = jnp.sum(jnp.where(hit, pos_in_expert, 0.0), axis=0, keepdims=True).astype(jnp.int32)
        wk_ref[k:k + 1, :] = jnp.sum(jnp.where(hit, gates, 0.0), axis=0, keepdims=True)


def _router(lay, x, g, mods, rw_t, rb):
    d = x.shape[1]
    n_exp = rw_t.shape[0]
    r = lay.R
    row = lambda i: (i, 0)
    tok = lambda i: (0, i)
    const = lambda i: (0, 0)
    tri_e = jnp.asarray(np.tril(np.ones((n_exp, n_exp), np.float32), -1)).astype(BF16)
    tri_t = jnp.asarray(np.triu(np.ones((r, r), np.float32), 1)).astype(BF16)
    return pl.pallas_call(
        functools.partial(_router_kernel, n_exp=n_exp),
        grid=(lay.ntiles,),
        in_specs=[
            pl.BlockSpec((r, d), row),
            _resident((1, d), const),
            _mod_spec(lay, d, 3),
            _mod_spec(lay, d, 4),
            _resident((n_exp, d), const),
            _resident((n_exp, 1), const),
            _resident((n_exp, n_exp), const),
            _resident((r, r), const),
        ],
        out_specs=[
            pl.BlockSpec((r, d // 2), row),
            pl.BlockSpec((TOP_K, r), tok),
            pl.BlockSpec((TOP_K, r), tok),
            pl.BlockSpec((TOP_K, r), tok),
            pl.BlockSpec((n_exp, 1), const),
        ],
        out_shape=[
            jax.ShapeDtypeStruct((lay.N, d // 2), jnp.uint32),
            jax.ShapeDtypeStruct((TOP_K, lay.N), jnp.int32),
            jax.ShapeDtypeStruct((TOP_K, lay.N), jnp.int32),
            jax.ShapeDtypeStruct((TOP_K, lay.N), F32),
            jax.ShapeDtypeStruct((n_exp, 1), F32),
        ],
        scratch_shapes=[pltpu.VMEM((n_exp, 1), F32)],
        compiler_params=_params(("arbitrary",)),
        name="moe_router",
    )(x, g, mods, mods, rw_t, rb, tri_e, tri_t)


def _dispatch_kernel(pad_start_ref, pad_len_ref, nused_ref, p_ref, f_ref, xs_ref, sem, zeros_ref,
                     *, n_exp, r, tmx):
    i = pl.program_id(0)

    def scatter_rows(t, carry):
        for k in range(TOP_K):
            p = p_ref[t * TOP_K + k]
            pltpu.make_async_copy(f_ref.at[pl.ds(t, 1)], xs_ref.at[pl.ds(p, 1)], sem.at[0]).start()
        return carry

    lax.fori_loop(0, r, scatter_rows, 0)
    pltpu.make_async_copy(xs_ref.at[pl.ds(0, r * TOP_K)], xs_ref.at[pl.ds(0, r * TOP_K)], sem.at[0]).wait()

    @pl.when(i == pl.num_programs(0) - 1)
    def _():
        zeros_ref[...] = jnp.zeros_like(zeros_ref)

        def pad_copies(e, act):
            start = pad_start_ref[e]
            length = pad_len_ref[e]
            head = jnp.minimum(length, (-start) & (V7X_SUBLANES - 1))
            base = start + head

            def row(j, c):
                act(pltpu.make_async_copy(zeros_ref.at[pl.ds(0, 1)], xs_ref.at[pl.ds(start + j, 1)], sem.at[1]))
                return c

            def group(j, c):
                off = pl.multiple_of(base + j * V7X_SUBLANES, V7X_SUBLANES)
                act(pltpu.make_async_copy(zeros_ref.at[pl.ds(0, V7X_SUBLANES)],
                                          xs_ref.at[pl.ds(off, V7X_SUBLANES)], sem.at[2]))
                return c

            lax.fori_loop(0, head, row, 0)
            lax.fori_loop(0, (length - head) // V7X_SUBLANES, group, 0)

        def tile_copy(m, act):
            dst = xs_ref.at[pl.ds(pl.multiple_of(m * tmx, tmx), tmx)]
            act(pltpu.make_async_copy(zeros_ref, dst, sem.at[3]))

        n_tiles = xs_ref.shape[0] // tmx
        for act in (lambda cp: cp.start(), lambda cp: cp.wait()):
            lax.fori_loop(0, n_exp, lambda e, c, act=act: (pad_copies(e, act), c)[1], 0)
            lax.fori_loop(nused_ref[0], n_tiles, lambda m, c, act=act: (tile_copy(m, act), c)[1], 0)


def _dispatch(lay, fpk, p_flat, pad_start, pad_len, n_used, n_rows, tmx):
    n, half = fpk.shape
    r = lay.R
    n_exp = pad_start.shape[0]
    grid_spec = pltpu.PrefetchScalarGridSpec(
        num_scalar_prefetch=3,
        grid=(lay.ntiles,),
        in_specs=[
            pl.BlockSpec((r * TOP_K,), lambda i, *_: (i,), memory_space=pltpu.SMEM),
            pl.BlockSpec((r, half), lambda i, *_: (i, 0)),
        ],
        out_specs=pl.BlockSpec(memory_space=pl.ANY),
        scratch_shapes=[pltpu.SemaphoreType.DMA((4,)), pltpu.VMEM((tmx, half), jnp.uint32)],
    )
    return pl.pallas_call(
        functools.partial(_dispatch_kernel, n_exp=n_exp, r=r, tmx=tmx),
        grid_spec=grid_spec,
        out_shape=jax.ShapeDtypeStruct((n_rows, half), jnp.uint32),
        compiler_params=_params(("arbitrary",)),
        name="moe_dispatch",
    )(pad_start, pad_len, n_used, p_flat, fpk)


def _swiglu_packed(xw, wg_ref, wu_ref, wd_ref):
    half = xw.shape[1]
    lo, hi = _unpack_pairs(xw)
    lo = lo.astype(BF16)
    hi = hi.astype(BF16)

    def proj(w_ref):
        top = jnp.dot(lo, w_ref[:half, :], preferred_element_type=F32)
        return top + jnp.dot(hi, w_ref[half:, :], preferred_element_type=F32)

    hid = (jax.nn.silu(proj(wg_ref)) * proj(wu_ref)).astype(BF16)
    return jnp.dot(hid, wd_ref[...], preferred_element_type=F32)


def _grouped_kernel(te_ref, nused_ref, xs_ref, wg_ref, wu_ref, wd_ref, ys_ref, wg_b, wu_b, wd_b):
    m = pl.program_id(0)

    @pl.when((m == 0) | (te_ref[m] != te_ref[jnp.maximum(m - 1, 0)]))
    def _():
        wg_b[...] = wg_ref[...].astype(BF16)
        wu_b[...] = wu_ref[...].astype(BF16)
        wd_b[...] = wd_ref[...].astype(BF16)

    @pl.when(m < nused_ref[0])
    def _():
        rows = xs_ref.shape[0] // EXPERT_SUBTILES
        for c in range(EXPERT_SUBTILES):
            rs = slice(c * rows, (c + 1) * rows)
            ys_ref[rs, :] = _pack_pairs(_swiglu_packed(xs_ref[rs, :], wg_b, wu_b, wd_b))

    @pl.when(m >= nused_ref[0])
    def _():
        ys_ref[...] = jnp.zeros_like(ys_ref)


def _grouped(xs, wg, wu, wd, layer, tile_expert, n_used, tmx):
    n_rows, half = xs.shape
    _, _, d, fdim = wg.shape
    grid_spec = pltpu.PrefetchScalarGridSpec(
        num_scalar_prefetch=2,
        grid=(n_rows // tmx,),
        in_specs=[
            pl.BlockSpec((tmx, half), lambda m, te, nu: (jnp.minimum(m, nu[0] - 1), 0)),
            pl.BlockSpec((None, None, d, fdim), lambda m, te, nu: (layer, te[m], 0, 0)),
            pl.BlockSpec((None, None, d, fdim), lambda m, te, nu: (layer, te[m], 0, 0)),
            pl.BlockSpec((None, None, fdim, d), lambda m, te, nu: (layer, te[m], 0, 0)),
        ],
        out_specs=pl.BlockSpec((tmx, half), lambda m, te, nu: (m, 0)),
        scratch_shapes=[pltpu.VMEM((d, fdim), BF16), pltpu.VMEM((d, fdim), BF16), pltpu.VMEM((fdim, d), BF16)],
    )
    return pl.pallas_call(
        _grouped_kernel,
        grid_spec=grid_spec,
        out_shape=jax.ShapeDtypeStruct((n_rows, half), jnp.uint32),
        compiler_params=_params(("arbitrary",)),
        name="moe_experts",
    )(tile_expert, n_used, xs, wg, wu, wd)


def _combine_kernel(p0_ref, pn_ref, w_ref, f_ref, x_ref, g2_ref, swg_ref, swu_ref, swd_ref, ys_ref, out_ref,
                    buf_a, buf_b, sh_ref, sem, *, r):
    i = pl.program_id(0)
    last = pl.num_programs(0) - 1
    half = f_ref.shape[1]
    rows8 = V7X_SUBLANES

    def issue(p_ref, dst, dsem, t0, count):
        for j in range(count):
            t = t0 + j
            for k in range(TOP_K):
                p = p_ref[t * TOP_K + k]
                pltpu.make_async_copy(ys_ref.at[pl.ds(p, 1)], dst.at[k, pl.ds(t, 1)], dsem).start()

    @pl.when(i == 0)
    def _():
        lax.fori_loop(0, r, lambda t, c: (issue(p0_ref, buf_a, sem.at[0], t, 1), c)[1], 0)

    def run(cur, cur_sem, nxt, nxt_sem):
        pltpu.make_async_copy(cur, cur, cur_sem).wait()
        sh_ref[...] = _swiglu_packed(f_ref[...], swg_ref, swu_ref, swd_ref)

        def finish_rows(g):
            rows = pl.ds(pl.multiple_of(g * rows8, rows8), rows8)
            acc_lo = sh_ref[rows, :half]
            acc_hi = sh_ref[rows, half:]
            w = w_ref[rows, :]
            for k in range(TOP_K):
                lo, hi = _unpack_pairs(cur[k, rows, :])
                wcol = w[:, k:k + 1]
                acc_lo = acc_lo + wcol * lo
                acc_hi = acc_hi + wcol * hi
            out_ref[rows, :half] = x_ref[rows, :half] + g2_ref[rows, :half] * acc_lo
            out_ref[rows, half:] = x_ref[rows, half:] + g2_ref[rows, half:] * acc_hi

        @pl.when(i < last)
        def _():
            def body(g, c):
                issue(pn_ref, nxt, nxt_sem, pl.multiple_of(g * rows8, rows8), rows8)
                finish_rows(g)
                return c

            lax.fori_loop(0, r // rows8, body, 0)

        @pl.when(i == last)
        def _():
            lax.fori_loop(0, r // rows8, lambda g, c: (finish_rows(g), c)[1], 0)

    @pl.when(i % 2 == 0)
    def _():
        run(buf_a, sem.at[0], buf_b, sem.at[1])

    @pl.when(i % 2 == 1)
    def _():
        run(buf_b, sem.at[1], buf_a, sem.at[0])


def _combine(lay, ys, p_flat, w_tok, fpk, x, mods, swg, swu, swd):
    n, d = x.shape
    half = d // 2
    r = lay.R
    fdim = swd.shape[0]
    row = lambda i: (i, 0)
    const = lambda i: (0, 0)
    gather_buf = pltpu.VMEM((TOP_K, r, half), jnp.uint32)
    return pl.pallas_call(
        functools.partial(_combine_kernel, r=r),
        grid=(lay.ntiles,),
        in_specs=[
            pl.BlockSpec((r * TOP_K,), lambda i: (i,), memory_space=pltpu.SMEM),
            pl.BlockSpec((r * TOP_K,), lambda i: (jnp.minimum(i + 1, lay.ntiles - 1),), memory_space=pltpu.SMEM),
            pl.BlockSpec((r, TOP_K), row),
            pl.BlockSpec((r, half), row),
            pl.BlockSpec((r, d), row),
            _mod_spec(lay, d, 5),
            _resident((d, fdim), const),
            _resident((d, fdim), const),
            _resident((fdim, d), const),
            pl.BlockSpec(memory_space=pl.ANY),
        ],
        out_specs=pl.BlockSpec((r, d), row),
        out_shape=jax.ShapeDtypeStruct((n, d), F32),
        scratch_shapes=[gather_buf, gather_buf, pltpu.VMEM((r, d), F32), pltpu.SemaphoreType.DMA((2,))],
        compiler_params=_params(("arbitrary",)),
        name="moe_combine",
    )(p_flat, p_flat, w_tok, fpk, x, mods, swg, swu, swd, ys)


def _moe(lay, x, g2n, mods, rw_t, rb, wg, wu, wd, layer, swg, swu, swd):
    n_exp = rw_t.shape[0]
    tmx = EXPERT_TILE_ROWS // lay.R * lay.R if lay.R <= EXPERT_TILE_ROWS else lay.R
    fpk, ek, rk, wk, cnt = _router(lay, x, g2n, mods, rw_t, rb)
    cnt = cnt[:, 0].astype(jnp.int32)
    padded = (cnt + tmx - 1) // tmx * tmx
    offs = jnp.concatenate([jnp.zeros((1,), jnp.int32), jnp.cumsum(padded)])
    n_tiles = lay.N * TOP_K // tmx + n_exp
    e_ids = jnp.arange(n_exp, dtype=jnp.int32)[:, None, None]
    seg_start = jnp.sum(jnp.where(ek[None] > e_ids, padded[:, None, None], 0), axis=0)
    p_flat = (seg_start + rk).T.reshape(-1)
    w_tok = wk.T
    n_used = (offs[n_exp] // tmx).reshape(1)
    tile_start = jnp.minimum(jnp.arange(n_tiles, dtype=jnp.int32), n_used[0] - 1) * tmx
    tile_expert = jnp.sum((tile_start[:, None] >= offs[None, 1:]).astype(jnp.int32), axis=1)
    xs = _dispatch(lay, fpk, p_flat, offs[:-1] + cnt, padded - cnt, n_used, n_tiles * tmx, tmx)
    ys = _grouped(xs, wg, wu, wd, layer, tile_expert, n_used, tmx)
    return _combine(lay, ys, p_flat, w_tok, fpk, x, mods, swg.astype(BF16), swu.astype(BF16), swd.astype(BF16))


def _final_norm_kernel(x_ref, g_ref, out_ref, s3, *, batch):
    x = x_ref[...]
    ms = jnp.mean(x * x, axis=-1, keepdims=True)
    y = x * lax.rsqrt(ms + EPS) * g_ref[...]
    for b, piece in enumerate(_split_by_batch(y, s3, batch)):
        out_ref[b] = piece


def _final_norm(lay, x, g):
    n, d = x.shape
    r, tt, batch = lay.R, lay.TT, lay.B
    return pl.pallas_call(
        functools.partial(_final_norm_kernel, batch=batch),
        grid=(n // r,),
        in_specs=[pl.BlockSpec((r, d), lambda i: (i, 0)), _resident((1, d), lambda i: (0, 0))],
        out_specs=pl.BlockSpec((batch, tt, d), lambda i: (0, i, 0)),
        out_shape=jax.ShapeDtypeStruct((batch, n // batch, d), F32),
        scratch_shapes=[pltpu.VMEM((d // V7X_LANES, r, V7X_LANES), F32)],
        compiler_params=_params(("arbitrary",)),
        name="final_norm",
    )(x, g)


def _grid_pos_embed(rows, dim):
    row = np.repeat(np.arange(rows, dtype=np.float32), GRID_W)
    col = np.tile(np.arange(GRID_W, dtype=np.float32), rows)
    quarter = dim // 4
    freqs = jnp.float32(POS_THETA) ** (-jnp.arange(quarter, dtype=F32) / quarter)
    ar = jnp.asarray(row)[:, None] * freqs
    ac = jnp.asarray(col)[:, None] * freqs
    return jnp.concatenate([jnp.sin(ar), jnp.cos(ar), jnp.sin(ac), jnp.cos(ac)], axis=-1).astype(F32)


def kernel(x, c, ctx, c_ctx, norm1_g, norm2_g, final_norm_g, w_mod, b_mod, rg_w_in, rg_conv_w, rg_conv_b, rg_w_a, rg_b_a, rg_w_i, rg_b_i, rg_lam, rg_w_out, fn_w_out, router_w, router_b, exp_w_gate, exp_w_up, exp_w_down, sh_w_gate, sh_w_up, sh_w_down):
    batch, t_lat, d = x.shape
    t_ctx = ctx.shape[1]
    depth = w_mod.shape[0]
    lay = _Layout(batch, t_ctx, t_lat, with_ctx=True)
    lay_lat = _Layout(batch, t_ctx, t_lat, with_ctx=False)

    n_cond = 1 + batch
    pad = (-n_cond) % V7X_SUBLANES
    cond = jnp.concatenate([c_ctx[None], c, jnp.zeros((pad, d), F32)], axis=0)
    mods_all = _ada(cond, w_mod, b_mod)

    pos = _grid_pos_embed(t_lat // GRID_W, d)
    xall = _addpos(lay, ctx, x, pos)

    gd = d // FN_GROUPS
    cg, sg = _dft_tables(gd)
    cs_tab = jnp.asarray(np.concatenate([cg, sg], axis=1)).astype(BF16)
    tab_lat = _dft_tables_split(t_lat)
    tab_ctx = _dft_tables_split(t_ctx)

    for i in range(depth):
        last = i == depth - 1
        j = i // N_MIXERS
        table = mods_all[i]
        mods = jnp.stack([jnp.broadcast_to(table[0:1], (lay.R, N_MOD * d)),
                          jnp.tile(table[1:1 + batch], (lay.TT, 1))])
        g1n = norm1_g[i][None]
        g2n = norm2_g[i][None]
        if i % N_MIXERS == 0:
            cur = lay
            gate, rec = _rg1(cur, xall, g1n, mods, rg_w_in[j].astype(BF16))
            hs = []
            for dr in range(2):
                hs.append(_rg2(cur, rec, rg_conv_w[j], rg_conv_b[j][None],
                               rg_w_a[j, dr].astype(BF16), rg_w_i[j, dr].astype(BF16),
                               rg_b_a[j, dr][None], rg_b_i[j, dr][None], rg_lam[j, dr][None],
                               rev=dr == 1))
            xall = _proj_res(cur, [hs[0], hs[1], gate], xall, mods, rg_w_out[j].astype(BF16))
        else:
            cur = lay_lat if last else lay
            x_off = lay.nct if last else 0
            fre = [_f2(_f1(lay, xall, g1n, mods, cs_tab, lay.nct, lay.nlt, latent=True), tab_lat)]
            if not last:
                fre = [_f2(_f1(lay, xall, g1n, mods, cs_tab, 0, lay.nct, latent=False), tab_ctx)] + fre
            xall = _proj_res(cur, fre, xall, mods, fn_w_out[j].astype(BF16), x_off=x_off)
        if last and cur is lay:
            xall = xall[lay.nct * lay.R:]
            cur = lay_lat
        xall = _moe(cur, xall, g2n, mods, router_w[i].T, router_b[i][:, None],
                    exp_w_gate, exp_w_up, exp_w_down, i, sh_w_gate[i], sh_w_up[i], sh_w_down[i])

    return _final_norm(lay_lat, xall, final_norm_g[None])
```
